```python
import jax, jax.numpy as jnp
from jax import lax
import numpy as np

D_MODEL = 1024
BATCH = 16
SEQ = 4096
DEPTH = 1

CHUNK = 64

D_POOL = D_MODEL // 2
POOL_WINDOWS = (2, 4, 8, 16)
POOL_GROUPS = len(POOL_WINDOWS)
POOL_GROUP_DIM = D_POOL // POOL_GROUPS

D_CONV = D_MODEL // 2
CONV_WIDTH = 3

D_IN = D_POOL + 3 * D_CONV + 2 * D_MODEL

N_KEYS = 128
N_EXPERTS = N_KEYS * N_KEYS
PEER_HEADS = 8
PEER_QUERY_DIM = 256
PEER_HALF = PEER_QUERY_DIM // 2
PEER_TOPK = 16
PEER_TOKEN_BLOCK = 128

RMS_EPS = 1e-6

kernel_name = "hybrid_pool_shortconv_peer_block"


def rmsnorm(x, g):
    xf = x.astype(jnp.float32)
    y = xf * lax.rsqrt(jnp.mean(xf * xf, axis=-1, keepdims=True) + RMS_EPS)
    return (y * g.astype(jnp.float32)).astype(x.dtype)


def pool_mixer(u, group_w, scale):
    S_ = u.shape[1]
    uf = u.astype(jnp.float32)
    csum = jnp.cumsum(uf, axis=1)
    pos = jnp.arange(S_)
    outs = []
    for g, w in enumerate(POOL_WINDOWS):
        sl = slice(g * POOL_GROUP_DIM, (g + 1) * POOL_GROUP_DIM)
        c = csum[:, :, sl]
        c_shift = jnp.pad(c[:, :S_ - w, :], ((0, 0), (w, 0), (0, 0)))
        cnt = jnp.minimum(pos + 1, w).astype(jnp.float32)[None, :, None]
        outs.append((c - c_shift) / cnt - uf[:, :, sl])
    p = jnp.stack(outs, axis=2).astype(u.dtype)
    p = jnp.einsum('bsgc,gcd->bsgd', p, group_w)
    return p.reshape(u.shape) * scale


def short_conv_mixer(c_gate, b_gate, v, conv_w):
    S_ = v.shape[1]
    u = c_gate * v
    up = jnp.pad(u, ((0, 0), (CONV_WIDTH - 1, 0), (0, 0)))
    y = up[:, 0:S_, :] * conv_w[0]
    for k in range(1, CONV_WIDTH):
        y = y + up[:, k:k + S_, :] * conv_w[k]
    return b_gate * y


def peer_ffn(h, w_q, sub_keys, expert_u, expert_v):
    B_, S_, D_ = h.shape
    hb_all = h.reshape(B_ * S_ // PEER_TOKEN_BLOCK, PEER_TOKEN_BLOCK, D_)

    def block(hb):
        q = (hb @ w_q).reshape(PEER_TOKEN_BLOCK, PEER_HEADS, 2, PEER_HALF)
        s = jnp.einsum('thpc,hpkc->thpk', q, sub_keys).astype(jnp.float32)
        top_s, top_i = lax.top_k(s, PEER_TOPK)
        cand_s = top_s[:, :, 0, :, None] + top_s[:, :, 1, None, :]
        cand_i = top_i[:, :, 0, :, None] * N_KEYS + top_i[:, :, 1, None, :]
        cand_s = cand_s.reshape(PEER_TOKEN_BLOCK, PEER_HEADS, PEER_TOPK * PEER_TOPK)
        cand_i = cand_i.reshape(PEER_TOKEN_BLOCK, PEER_HEADS, PEER_TOPK * PEER_TOPK)
        fin_s, fin_pos = lax.top_k(cand_s, PEER_TOPK)
        idx = jnp.take_along_axis(cand_i, fin_pos, axis=-1)
        gates = jax.nn.softmax(fin_s, axis=-1)
        u = expert_u[idx]
        act = jax.nn.gelu(jnp.einsum('thkd,td->thk', u, hb).astype(jnp.float32), approximate=False)
        wts = (gates * act).astype(hb.dtype)
        return jnp.einsum('thk,thkd->td', wts, expert_v[idx])

    out = lax.map(block, hb_all)
    return out.reshape(B_, S_, D_)


def setup_inputs(seed: int = 0) -> dict:
    key = jax.random.key(seed)
    ks = jax.random.split(key, 16)
    f32 = jnp.float32
    nrm = lambda k, shape, s: jax.random.normal(k, shape, f32) * s
    L = DEPTH
    return {
        "x": nrm(ks[0], (BATCH, SEQ, D_MODEL), 1.0),
        "g_mix": 1.0 + nrm(ks[1], (L, D_MODEL), 0.02),
        "w_in": nrm(ks[2], (L, D_MODEL, D_IN), D_MODEL ** -0.5),
        "pool_group_w": nrm(ks[3], (L, POOL_GROUPS, POOL_GROUP_DIM, POOL_GROUP_DIM), POOL_GROUP_DIM ** -0.5),
        "pool_scale": 1.0 + nrm(ks[4], (L, D_POOL), 0.02),
        "conv_w": nrm(ks[5], (L, CONV_WIDTH, D_CONV), CONV_WIDTH ** -0.5),
        "w_branch_pool": nrm(ks[6], (L, D_POOL, D_MODEL), D_POOL ** -0.5),
        "w_branch_conv": nrm(ks[7], (L, D_CONV, D_MODEL), D_CONV ** -0.5),
        "w_out": nrm(ks[8], (L, D_MODEL, D_MODEL), D_MODEL ** -0.5),
        "g_ffn": 1.0 + nrm(ks[9], (L, D_MODEL), 0.02),
        "w_q": nrm(ks[10], (L, D_MODEL, PEER_HEADS * PEER_QUERY_DIM), D_MODEL ** -0.5),
        "sub_keys": nrm(ks[11], (L, PEER_HEADS, 2, N_KEYS, PEER_HALF), PEER_HALF ** -0.5),
        "expert_u": nrm(ks[12], (L, N_EXPERTS, D_MODEL), D_MODEL ** -0.5),
        "expert_v": nrm(ks[13], (L, N_EXPERTS, D_MODEL), PEER_HEADS ** -0.5),
        "g_final": 1.0 + nrm(ks[14], (D_MODEL,), 0.02),
    }


def reference(x, g_mix, w_in, pool_group_w, pool_scale, conv_w, w_branch_pool, w_branch_conv,
              w_out, g_ffn, w_q, sub_keys, expert_u, expert_v, g_final):
    for l in range(DEPTH):
        h = rmsnorm(x, g_mix[l])
        z = h @ w_in[l]
        o1 = D_POOL
        o2 = o1 + D_CONV
        o3 = o2 + D_CONV
        o4 = o3 + D_CONV
        o5 = o4 + D_MODEL
        z_pool, z_c, z_b, z_v = z[..., :o1], z[..., o1:o2], z[..., o2:o3], z[..., o3:o4]
        gate_a, gate_b = z[..., o4:o5], z[..., o5:]
        a = pool_mixer(z_pool, pool_group_w[l], pool_scale[l]) @ w_branch_pool[l]
        b = short_conv_mixer(z_c, z_b, z_v, conv_w[l]) @ w_branch_conv[l]
        m = jax.nn.sigmoid(gate_a) * a + jax.nn.sigmoid(gate_b) * b
        x = x + m @ w_out[l]
        h2 = rmsnorm(x, g_ffn[l])
        x = x + peer_ffn(h2, w_q[l], sub_keys[l], expert_u[l], expert_v[l])
    return rmsnorm(x, g_final)
```

```python
import functools
import math

import jax
import jax.numpy as jnp
from jax import lax
from jax.experimental import pallas as pl
from jax.experimental.pallas import tpu as pltpu

D_MODEL = 1024
D_POOL = 512
POOL_WINDOWS = (2, 4, 8, 16)
POOL_GROUP_DIM = 128
D_CONV = 512
CONV_WIDTH = 3
N_KEYS = 128
N_EXPERTS = N_KEYS * N_KEYS
PEER_HEADS = 8
PEER_HALF = 128
PEER_TOPK = 16
SLOTS = PEER_HEADS * PEER_TOPK
RMS_EPS = 1e-6

LANES = 128
HALF_D = D_MODEL // 2
ROW_SUBLANES = HALF_D // LANES

MIX_TILE = 512
POOL_HIST = 16
CONV_HIST = 8
ROUTE_TILE = 256
GATHER_TILE = 128
FINAL_TILE = 1024
VMEM_LIMIT = 56 * 1024 * 1024

F32 = jnp.float32
BF16 = jnp.bfloat16


def _rms(x, g):
    return x * lax.rsqrt(jnp.mean(x * x, axis=-1, keepdims=True) + RMS_EPS) * g


def _dot(a, b):
    return jnp.dot(a, b, preferred_element_type=F32)


def _mixer_kernel(x_ref, g_ref, win_ref, pgw_ref, pscale_ref, convw_ref, wbp_ref, wbc_ref, wout_ref,
                  x1_ref, zp_buf, u_buf):
    s = pl.program_id(1)
    ts = MIX_TILE

    @pl.when(s == 0)
    def _():
        zp_buf[0:POOL_HIST, :] = jnp.zeros((POOL_HIST, D_POOL), F32)
        u_buf[0:CONV_HIST, :] = jnp.zeros((CONV_HIST, D_CONV), F32)

    x = x_ref[...]
    hb = _rms(x, g_ref[...]).astype(BF16)

    zp = _dot(hb, win_ref[:, 0:D_POOL])
    zp_buf[POOL_HIST:POOL_HIST + ts, :] = zp
    pos = s * ts + lax.broadcasted_iota(jnp.int32, (ts, 1), 0)
    outs = []
    for g, w in enumerate(POOL_WINDOWS):
        c0 = g * POOL_GROUP_DIM
        cur = zp[:, c0:c0 + POOL_GROUP_DIM]
        acc = cur
        for k in range(1, w):
            acc = acc + zp_buf[POOL_HIST - k:POOL_HIST - k + ts, c0:c0 + POOL_GROUP_DIM]
        cnt = jnp.minimum(pos + 1, w).astype(F32)
        pg = acc / cnt - cur
        og = _dot(pg.astype(BF16), pgw_ref[g]) * pscale_ref[:, c0:c0 + POOL_GROUP_DIM]
        outs.append(og)
    pm = jnp.concatenate(outs, axis=1)
    a = _dot(pm.astype(BF16), wbp_ref[...])
    zp_buf[0:POOL_HIST, :] = zp_buf[ts:ts + POOL_HIST, :]

    o = D_POOL
    zc = _dot(hb, win_ref[:, o:o + D_CONV])
    zb = _dot(hb, win_ref[:, o + D_CONV:o + 2 * D_CONV])
    zv = _dot(hb, win_ref[:, o + 2 * D_CONV:o + 3 * D_CONV])
    u = zc * zv
    u_buf[CONV_HIST:CONV_HIST + ts, :] = u
    y = (u_buf[CONV_HIST - 2:CONV_HIST - 2 + ts, :] * convw_ref[0:1, :]
         + u_buf[CONV_HIST - 1:CONV_HIST - 1 + ts, :] * convw_ref[1:2, :]
         + u * convw_ref[2:3, :])
    b = _dot((zb * y).astype(BF16), wbc_ref[...])
    u_buf[0:CONV_HIST, :] = u_buf[ts:ts + CONV_HIST, :]

    o = D_POOL + 3 * D_CONV
    ga = _dot(hb, win_ref[:, o:o + D_MODEL])
    gb = _dot(hb, win_ref[:, o + D_MODEL:o + 2 * D_MODEL])
    m = jax.nn.sigmoid(ga) * a + jax.nn.sigmoid(gb) * b
    x1_ref[...] = x + _dot(m.astype(BF16), wout_ref[...])


def _const_spec(shape):
    nd = len(shape)
    return pl.BlockSpec(shape, lambda *_: (0,) * nd, pipeline_mode=pl.Buffered(1))


def _mixer(x2d, g_mix, w_in, pgw, pscale, conv_w, wbp, wbc, w_out, batch, seq):
    n = batch * seq
    n_s = seq // MIX_TILE
    d_in = w_in.shape[1]
    return pl.pallas_call(
        _mixer_kernel,
        grid=(batch, n_s),
        in_specs=[
            pl.BlockSpec((MIX_TILE, D_MODEL), lambda b, s: (b * n_s + s, 0)),
            _const_spec((1, D_MODEL)),
            _const_spec((D_MODEL, d_in)),
            _const_spec((len(POOL_WINDOWS), POOL_GROUP_DIM, POOL_GROUP_DIM)),
            _const_spec((1, D_POOL)),
            _const_spec((CONV_WIDTH, D_CONV)),
            _const_spec((D_POOL, D_MODEL)),
            _const_spec((D_CONV, D_MODEL)),
            _const_spec((D_MODEL, D_MODEL)),
        ],
        out_specs=pl.BlockSpec((MIX_TILE, D_MODEL), lambda b, s: (b * n_s + s, 0)),
        out_shape=jax.ShapeDtypeStruct((n, D_MODEL), F32),
        scratch_shapes=[pltpu.VMEM((POOL_HIST + MIX_TILE, D_POOL), F32),
                        pltpu.VMEM((CONV_HIST + MIX_TILE, D_CONV), F32)],
        compiler_params=pltpu.CompilerParams(dimension_semantics=("arbitrary", "arbitrary"),
                                             vmem_limit_bytes=VMEM_LIMIT),
        name="mixer",
    )(x2d, g_mix, w_in, pgw, pscale, conv_w, wbp, wbc, w_out)


def _topk_axis0(s, k, payload=None):
    n = s.shape[0]
    iota = lax.broadcasted_iota(jnp.int32, s.shape, 0)
    vals, poss, pays = [], [], []
    for _ in range(k):
        m = jnp.max(s, axis=0, keepdims=True)
        pos = jnp.min(jnp.where(s == m, iota, n), axis=0, keepdims=True)
        hit = iota == pos
        vals.append(m)
        poss.append(pos)
        if payload is not None:
            pays.append(jnp.max(jnp.where(hit, payload, -1), axis=0, keepdims=True))
        s = jnp.where(hit, -jnp.inf, s)
    vals = jnp.concatenate(vals, axis=0)
    poss = jnp.concatenate(poss, axis=0)
    if payload is None:
        return vals, poss, None
    return vals, poss, jnp.concatenate(pays, axis=0)


def _route_kernel(x1_ref, g_ref, wq_ref, keys_ref, h2_ref, idx_ref, gate_ref, hb_buf):
    h = pl.program_id(1)

    @pl.when(h == 0)
    def _():
        h2 = _rms(x1_ref[...], g_ref[...])
        h2_ref[...] = h2
        hb_buf[...] = h2.astype(BF16)

    q = _dot(hb_buf[...], wq_ref[...]).astype(BF16)
    for c in range(ROUTE_TILE // LANES):
        tok = slice(c * LANES, (c + 1) * LANES)
        top_s, top_i = [], []
        for p in range(2):
            qs = q[tok, p * PEER_HALF:(p + 1) * PEER_HALF]
            st = lax.dot_general(keys_ref[0, p], qs, (((1,), (1,)), ((), ())),
                                 preferred_element_type=F32)
            v, i, _ = _topk_axis0(st, PEER_TOPK)
            top_s.append(v)
            top_i.append(i)
        cand_s = jnp.concatenate([top_s[0][i:i + 1, :] + top_s[1] for i in range(PEER_TOPK)], axis=0)
        cand_i = jnp.concatenate([top_i[0][i:i + 1, :] * N_KEYS + top_i[1] for i in range(PEER_TOPK)],
                                 axis=0)
        fin_s, _, fin_i = _topk_axis0(cand_s, PEER_TOPK, payload=cand_i)
        e = jnp.exp(fin_s - fin_s[0:1, :])
        gates = e / jnp.sum(e, axis=0, keepdims=True)
        idx_ref[0, :, tok] = fin_i
        gate_ref[0, :, tok] = gates


def _route(x1, g_ffn, w_q, keys):
    n = x1.shape[0]
    nb = n // ROUTE_TILE
    return pl.pallas_call(
        _route_kernel,
        grid=(nb, PEER_HEADS),
        in_specs=[
            pl.BlockSpec((ROUTE_TILE, D_MODEL), lambda i, h: (i, 0)),
            pl.BlockSpec((1, D_MODEL), lambda i, h: (0, 0)),
            pl.BlockSpec((D_MODEL, 2 * PEER_HALF), lambda i, h: (0, h)),
            pl.BlockSpec((1, 2, N_KEYS, PEER_HALF), lambda i, h: (h, 0, 0, 0)),
        ],
        out_specs=[
            pl.BlockSpec((ROUTE_TILE, D_MODEL), lambda i, h: (i, 0)),
            pl.BlockSpec((1, PEER_TOPK, ROUTE_TILE), lambda i, h: (i, h, 0)),
            pl.BlockSpec((1, PEER_TOPK, ROUTE_TILE), lambda i, h: (i, h, 0)),
        ],
        out_shape=[
            jax.ShapeDtypeStruct((n, D_MODEL), F32),
            jax.ShapeDtypeStruct((nb, SLOTS, ROUTE_TILE), jnp.int32),
            jax.ShapeDtypeStruct((nb, SLOTS, ROUTE_TILE), F32),
        ],
        scratch_shapes=[pltpu.VMEM((ROUTE_TILE, D_MODEL), BF16)],
        compiler_params=pltpu.CompilerParams(dimension_semantics=("arbitrary", "arbitrary"),
                                             vmem_limit_bytes=VMEM_LIMIT),
        name="route",
    )(x1, g_ffn, w_q, keys)


def _pack_table(w):
    bits = lax.bitcast_convert_type(w.astype(BF16), jnp.uint16).astype(jnp.uint32)
    word = bits[:, :HALF_D] | (bits[:, HALF_D:] << 16)
    return word.reshape(w.shape[0], ROW_SUBLANES, LANES)


def _unpack(row):
    lo = lax.bitcast_convert_type(row << 16, F32)
    hi = lax.bitcast_convert_type(row & jnp.uint32(0xFFFF0000), F32)
    return lo, hi


def _gelu(x):
    return 0.5 * x * (1.0 + lax.erf(x * (1.0 / math.sqrt(2.0))))


def _acts_kernel(idx_ref, h_ref, gate_ref, tab_ref, w_ref, part_buf):
    lane = lax.broadcasted_iota(jnp.int32, (SLOTS, GATHER_TILE), 1)

    def token(t, acc):
        h_lo = h_ref[t, 0]
        h_hi = h_ref[t, 1]
        for j in range(SLOTS):
            lo, hi = _unpack(tab_ref[idx_ref[0, j, t]])
            p = lo * h_lo + hi * h_hi
            part_buf[j:j + 1, :] = jnp.sum(p, axis=0, keepdims=True)
        col = jnp.sum(part_buf[...], axis=1, keepdims=True)
        return jnp.where(lane == t, col, acc)

    act = lax.fori_loop(0, GATHER_TILE, token, jnp.zeros((SLOTS, GATHER_TILE), F32))
    w_ref[0] = gate_ref[0] * _gelu(act)


def _acts(idx, h2r, gates, table):
    nb = idx.shape[0]
    return pl.pallas_call(
        _acts_kernel,
        grid=(nb,),
        in_specs=[
            pl.BlockSpec((1, SLOTS, GATHER_TILE), lambda i: (i, 0, 0), memory_space=pltpu.SMEM),
            pl.BlockSpec((GATHER_TILE, 2, ROW_SUBLANES, LANES), lambda i: (i, 0, 0, 0)),
            pl.BlockSpec((1, SLOTS, GATHER_TILE), lambda i: (i, 0, 0)),
            pl.BlockSpec(memory_space=pltpu.VMEM),
        ],
        out_specs=pl.BlockSpec((1, SLOTS, GATHER_TILE), lambda i: (i, 0, 0)),
        out_shape=jax.ShapeDtypeStruct((nb, SLOTS, GATHER_TILE), F32),
        scratch_shapes=[pltpu.VMEM((SLOTS, LANES), F32)],
        compiler_params=pltpu.CompilerParams(dimension_semantics=("arbitrary",),
                                             vmem_limit_bytes=VMEM_LIMIT),
        name="acts",
    )(idx, h2r, gates, table)


def _combine_kernel(idx_ref, w_ref, tab_ref, out_ref):
    chains = 4

    def token(t, carry):
        acc_lo = [jnp.zeros((ROW_SUBLANES, LANES), F32) for _ in range(chains)]
        acc_hi = [jnp.zeros((ROW_SUBLANES, LANES), F32) for _ in range(chains)]
        for j in range(SLOTS):
            lo, hi = _unpack(tab_ref[idx_ref[0, j, t]])
            w = w_ref[0, j, t]
            acc_lo[j % chains] = acc_lo[j % chains] + w * lo
            acc_hi[j % chains] = acc_hi[j % chains] + w * hi
        out_ref[t, 0:ROW_SUBLANES, :] = (acc_lo[0] + acc_lo[1]) + (acc_lo[2] + acc_lo[3])
        out_ref[t, ROW_SUBLANES:2 * ROW_SUBLANES, :] = (acc_hi[0] + acc_hi[1]) + (acc_hi[2] + acc_hi[3])
        return carry

    lax.fori_loop(0, GATHER_TILE, token, 0)


def _combine(idx, w, table):
    nb = idx.shape[0]
    n = nb * GATHER_TILE
    return pl.pallas_call(
        _combine_kernel,
        grid=(nb,),
        in_specs=[
            pl.BlockSpec((1, SLOTS, GATHER_TILE), lambda i: (i, 0, 0), memory_space=pltpu.SMEM),
            pl.BlockSpec((1, SLOTS, GATHER_TILE), lambda i: (i, 0, 0), memory_space=pltpu.SMEM),
            pl.BlockSpec(memory_space=pltpu.VMEM),
        ],
        out_specs=pl.BlockSpec((GATHER_TILE, 2 * ROW_SUBLANES, LANES), lambda i: (i, 0, 0)),
        out_shape=jax.ShapeDtypeStruct((n, 2 * ROW_SUBLANES, LANES), F32),
        compiler_params=pltpu.CompilerParams(dimension_semantics=("arbitrary",),
                                             vmem_limit_bytes=VMEM_LIMIT),
        name="combine",
    )(idx, w, table)


def _final_kernel(x1_ref, peer_ref, g_ref, out_ref):
    out_ref[...] = _rms(x1_ref[...] + peer_ref[...], g_ref[...])


def _final(x1, peer, g_final):
    n = x1.shape[0]
    spec = pl.BlockSpec((FINAL_TILE, D_MODEL), lambda i: (i, 0))
    return pl.pallas_call(
        _final_kernel,
        grid=(n // FINAL_TILE,),
        in_specs=[spec, spec, pl.BlockSpec((1, D_MODEL), lambda i: (0, 0))],
        out_specs=spec,
        out_shape=jax.ShapeDtypeStruct((n, D_MODEL), F32),
        compiler_params=pltpu.CompilerParams(dimension_semantics=("arbitrary",),
                                             vmem_limit_bytes=VMEM_LIMIT),
        name="final",
    )(x1, peer, g_final)


def _regroup(a, tile_from, tile_to):
    nb = a.shape[0]
    flat = a.transpose(1, 0, 2).reshape(SLOTS, nb * tile_from)
    return flat.reshape(SLOTS, nb * tile_from // tile_to, tile_to).transpose(1, 0, 2)


def kernel(x, g_mix, w_in, pool_group_w, pool_scale, conv_w, w_branch_pool, w_branch_conv, w_out, g_ffn,
           w_q, sub_keys, expert_u, expert_v, g_final):
    batch, seq, d = x.shape
    n = batch * seq
    assert g_mix.shape[0] == 1, "the final residual add is fused with the final norm: one layer only"
    l = 0
    x1 = _mixer(x.reshape(n, d), g_mix[l][None, :], w_in[l].astype(BF16), pool_group_w[l].astype(BF16),
                pool_scale[l][None, :], conv_w[l], w_branch_pool[l].astype(BF16),
                w_branch_conv[l].astype(BF16), w_out[l].astype(BF16), batch, seq)
    h2, idx, gates = _route(x1, g_ffn[l][None, :], w_q[l].astype(BF16), sub_keys[l].astype(BF16))
    if ROUTE_TILE != GATHER_TILE:
        idx = _regroup(idx, ROUTE_TILE, GATHER_TILE)
        gates = _regroup(gates, ROUTE_TILE, GATHER_TILE)
    h2r = h2.reshape(n, 2, ROW_SUBLANES, LANES)
    wts = _acts(idx, h2r, gates, _pack_table(expert_u[l]))
    peer = _combine(idx, wts, _pack_table(expert_v[l]))
    return _final(x1, peer.reshape(n, d), g_final[None, :]).reshape(batch, seq, d)
```

```python
import functools
import math

import jax
import jax.numpy as jnp
from jax import lax
from jax.experimental import pallas as pl
from jax.experimental.pallas import tpu as pltpu

D_MODEL = 1024
D_POOL = 512
POOL_WINDOWS = (2, 4, 8, 16)
POOL_GROUP_DIM = 128
D_CONV = 512
CONV_WIDTH = 3
N_KEYS = 128
N_EXPERTS = N_KEYS * N_KEYS
PEER_HEADS = 8
PEER_HALF = 128
PEER_TOPK = 16
SLOTS = PEER_HEADS * PEER_TOPK
RMS_EPS = 1e-6

LANES = 128
HALF_D = D_MODEL // 2
ROW_SUBLANES = HALF_D // LANES

MIX_TILE = 512
POOL_HIST = 16
CONV_HIST = 8
ROUTE_TILE = 256
GATHER_TILE = 128
FINAL_TILE = 1024
VMEM_LIMIT = 56 * 1024 * 1024

F32 = jnp.float32
BF16 = jnp.bfloat16


def _rms(x, g):
    return x * lax.rsqrt(jnp.mean(x * x, axis=-1, keepdims=True) + RMS_EPS) * g


def _dot(a, b):
    return jnp.dot(a, b, preferred_element_type=F32)


def _mixer_kernel(x_ref, g_ref, win_ref, pgw_ref, pscale_ref, convw_ref, wbp_ref, wbc_ref, wout_ref,
                  x1_ref, zp_buf, u_buf):
    s = pl.program_id(1)
    ts = MIX_TILE

    @pl.when(s == 0)
    def _():
        zp_buf[0:POOL_HIST, :] = jnp.zeros((POOL_HIST, D_POOL), F32)
        u_buf[0:CONV_HIST, :] = jnp.zeros((CONV_HIST, D_CONV), F32)

    x = x_ref[...]
    hb = _rms(x, g_ref[...]).astype(BF16)

    zp = _dot(hb, win_ref[:, 0:D_POOL])
    zp_buf[POOL_HIST:POOL_HIST + ts, :] = zp
    pos = s * ts + lax.broadcasted_iota(jnp.int32, (ts, 1), 0)
    outs = []
    for g, w in enumerate(POOL_WINDOWS):
        c0 = g * POOL_GROUP_DIM
        cur = zp[:, c0:c0 + POOL_GROUP_DIM]
        acc = cur
        for k in range(1, w):
            acc = acc + zp_buf[POOL_HIST - k:POOL_HIST - k + ts, c0:c0 + POOL_GROUP_DIM]
        cnt = jnp.minimum(pos + 1, w).astype(F32)
        pg = acc / cnt - cur
        og = _dot(pg.astype(BF16), pgw_ref[g]) * pscale_ref[:, c0:c0 + POOL_GROUP_DIM]
        outs.append(og)
    pm = jnp.concatenate(outs, axis=1)
    a = _dot(pm.astype(BF16), wbp_ref[...])
    zp_buf[0:POOL_HIST, :] = zp_buf[ts:ts + POOL_HIST, :]

    o = D_POOL
    zc = _dot(hb, win_ref[:, o:o + D_CONV])
    zb = _dot(hb, win_ref[:, o + D_CONV:o + 2 * D_CONV])
    zv = _dot(hb, win_ref[:, o + 2 * D_CONV:o + 3 * D_CONV])
    u = zc * zv
    u_buf[CONV_HIST:CONV_HIST + ts, :] = u
    y = (u_buf[CONV_HIST - 2:CONV_HIST - 2 + ts, :] * convw_ref[0:1, :]
         + u_buf[CONV_HIST - 1:CONV_HIST - 1 + ts, :] * convw_ref[1:2, :]
         + u * convw_ref[2:3, :])
    b = _dot((zb * y).astype(BF16), wbc_ref[...])
    u_buf[0:CONV_HIST, :] = u_buf[ts:ts + CONV_HIST, :]

    o = D_POOL + 3 * D_CONV
    ga = _dot(hb, win_ref[:, o:o + D_MODEL])
    gb = _dot(hb, win_ref[:, o + D_MODEL:o + 2 * D_MODEL])
    m = jax.nn.sigmoid(ga) * a + jax.nn.sigmoid(gb) * b
    x1_ref[...] = x + _dot(m.astype(BF16), wout_ref[...])


def _const_spec(shape):
    nd = len(shape)
    return pl.BlockSpec(shape, lambda *_: (0,) * nd, pipeline_mode=pl.Buffered(1))


def _mixer(x2d, g_mix, w_in, pgw, pscale, conv_w, wbp, wbc, w_out, batch, seq):
    n = batch * seq
    n_s = seq // MIX_TILE
    d_in = w_in.shape[1]
    return pl.pallas_call(
        _mixer_kernel,
        grid=(batch, n_s),
        in_specs=[
            pl.BlockSpec((MIX_TILE, D_MODEL), lambda b, s: (b * n_s + s, 0)),
            _const_spec((1, D_MODEL)),
            _const_spec((D_MODEL, d_in)),
            _const_spec((len(POOL_WINDOWS), POOL_GROUP_DIM, POOL_GROUP_DIM)),
            _const_spec((1, D_POOL)),
            _const_spec((CONV_WIDTH, D_CONV)),
            _const_spec((D_POOL, D_MODEL)),
            _const_spec((D_CONV, D_MODEL)),
            _const_spec((D_MODEL, D_MODEL)),
        ],
        out_specs=pl.BlockSpec((MIX_TILE, D_MODEL), lambda b, s: (b * n_s + s, 0)),
        out_shape=jax.ShapeDtypeStruct((n, D_MODEL), F32),
        scratch_shapes=[pltpu.VMEM((POOL_HIST + MIX_TILE, D_POOL), F32),
                        pltpu.VMEM((CONV_HIST + MIX_TILE, D_CONV), F32)],
        compiler_params=pltpu.CompilerParams(dimension_semantics=("arbitrary", "arbitrary"),
                                             vmem_limit_bytes=VMEM_LIMIT),
        name="mixer",
    )(x2d, g_mix, w_in, pgw, pscale, conv_w, wbp, wbc, w_out)


def _topk_axis0(s, k, pos, payload=None):
    big = jnp.int32(2 ** 30)
    vals, sel = [], []
    for _ in range(k):
        m = jnp.max(s, axis=0, keepdims=True)
        p = jnp.min(jnp.where(s == m, pos, big), axis=0, keepdims=True)
        hit = pos == p
        vals.append(m)
        if payload is None:
            sel.append(p)
        else:
            sel.append(jnp.max(jnp.where(hit, payload, -1), axis=0, keepdims=True))
        s = jnp.where(hit, -jnp.inf, s)
    return jnp.concatenate(vals, axis=0), jnp.concatenate(sel, axis=0)


_CAND_GROUPS = ((range(0, 1), range(0, 16)),) + tuple((range(i, i + 1), range(0, 8)) for i in range(1, 8)) \
    + ((range(8, 16), range(0, 1)),)


def _candidates(top_s, top_i):
    t = top_s[0].shape[1]
    cs, cp, ci = [], [], []
    for i_rng, j_rng in _CAND_GROUPS:
        i0, i1, j0, j1 = i_rng.start, i_rng.stop, j_rng.start, j_rng.stop
        rows = max(i1 - i0, j1 - j0)
        r = lax.broadcasted_iota(jnp.int32, (rows, t), 0)
        cs.append(top_s[0][i0:i1, :] + top_s[1][j0:j1, :])
        ci.append(top_i[0][i0:i1, :] * N_KEYS + top_i[1][j0:j1, :])
        cp.append(r * PEER_TOPK + i0 * PEER_TOPK + j0 if i1 - i0 > 1 else r + i0 * PEER_TOPK + j0)
    return jnp.concatenate(cs, axis=0), jnp.concatenate(cp, axis=0), jnp.concatenate(ci, axis=0)


def _route_kernel(x1_ref, g_ref, wq_ref, keys_ref, h2_ref, idx_ref, gate_ref, hb_buf):
    h = pl.program_id(1)

    @pl.when(h == 0)
    def _():
        h2 = _rms(x1_ref[...], g_ref[...])
        h2_ref[...] = h2
        hb_buf[...] = h2.astype(BF16)

    q = _dot(hb_buf[...], wq_ref[...]).astype(BF16)
    for c in range(ROUTE_TILE // LANES):
        tok = slice(c * LANES, (c + 1) * LANES)
        key_pos = lax.broadcasted_iota(jnp.int32, (N_KEYS, LANES), 0)
        top_s, top_i = [], []
        for p in range(2):
            qs = q[tok, p * PEER_HALF:(p + 1) * PEER_HALF]
            st = lax.dot_general(keys_ref[0, p], qs, (((1,), (1,)), ((), ())),
                                 preferred_element_type=F32)
            v, i = _topk_axis0(st, PEER_TOPK, key_pos)
            top_s.append(v)
            top_i.append(i)
        cand_s, cand_pos, cand_i = _candidates(top_s, top_i)
        fin_s, fin_i = _topk_axis0(cand_s, PEER_TOPK, cand_pos, payload=cand_i)
        e = jnp.exp(fin_s - fin_s[0:1, :])
        gates = e / jnp.sum(e, axis=0, keepdims=True)
        idx_ref[0, :, tok] = fin_i
        gate_ref[0, :, tok] = gates


def _route(x1, g_ffn, w_q, keys):
    n = x1.shape[0]
    nb = n // ROUTE_TILE
    return pl.pallas_call(
        _route_kernel,
        grid=(nb, PEER_HEADS),
        in_specs=[
            pl.BlockSpec((ROUTE_TILE, D_MODEL), lambda i, h: (i, 0)),
            pl.BlockSpec((1, D_MODEL), lambda i, h: (0, 0)),
            pl.BlockSpec((D_MODEL, 2 * PEER_HALF), lambda i, h: (0, h)),
            pl.BlockSpec((1, 2, N_KEYS, PEER_HALF), lambda i, h: (h, 0, 0, 0)),
        ],
        out_specs=[
            pl.BlockSpec((ROUTE_TILE, D_MODEL), lambda i, h: (i, 0)),
            pl.BlockSpec((1, PEER_TOPK, ROUTE_TILE), lambda i, h: (i, h, 0)),
            pl.BlockSpec((1, PEER_TOPK, ROUTE_TILE), lambda i, h: (i, h, 0)),
        ],
        out_shape=[
            jax.ShapeDtypeStruct((n, D_MODEL), F32),
            jax.ShapeDtypeStruct((nb, SLOTS, ROUTE_TILE), jnp.int32),
            jax.ShapeDtypeStruct((nb, SLOTS, ROUTE_TILE), F32),
        ],
        scratch_shapes=[pltpu.VMEM((ROUTE_TILE, D_MODEL), BF16)],
        compiler_params=pltpu.CompilerParams(dimension_semantics=("arbitrary", "arbitrary"),
                                             vmem_limit_bytes=VMEM_LIMIT),
        name="route",
    )(x1, g_ffn, w_q, keys)


def _pack_table(w):
    bits = lax.bitcast_convert_type(w.astype(BF16), jnp.uint16).astype(jnp.uint32)
    word = bits[:, :HALF_D] | (bits[:, HALF_D:] << 16)
    return word.reshape(w.shape[0] * ROW_SUBLANES, LANES)


def _unpack(row):
    lo = lax.bitcast_convert_type(row << 16, F32)
    hi = lax.bitcast_convert_type(row & jnp.uint32(0xFFFF0000), F32)
    return lo, hi


def _gelu(x):
    return 0.5 * x * (1.0 + lax.erf(x * (1.0 / math.sqrt(2.0))))


SUBLANES = 2 * ROW_SUBLANES


IDX_GROUP = 8
IDX_ARRAYS = SLOTS // IDX_GROUP


def _pair_rows(tab_ref, idx_refs, t, ja, jb):
    def row(j):
        off = idx_refs[j // IDX_GROUP][t * IDX_GROUP + j % IDX_GROUP]
        return tab_ref[pl.ds(pl.multiple_of(off, ROW_SUBLANES), ROW_SUBLANES), :]
    return jnp.concatenate([row(ja), row(jb)], axis=0)


def _half_sums(p0, p1, p2, p3, sub):
    in_lo2 = (sub & 3) < 2
    even = (sub & 1) == 0

    def fold2(a, b):
        return (jnp.where(in_lo2, a, pltpu.roll(b, 2, 0))
                + jnp.where(in_lo2, pltpu.roll(a, SUBLANES - 2, 0), b))

    y02, y13 = fold2(p0, p2), fold2(p1, p3)
    return (jnp.where(even, y02, pltpu.roll(y13, 1, 0))
            + jnp.where(even, pltpu.roll(y02, SUBLANES - 1, 0), y13))


def _acts_kernel(*refs):
    idx_refs, (h_ref, gate_ref, tab_ref, w_ref) = refs[:IDX_ARRAYS], refs[IDX_ARRAYS:]
    lane = lax.broadcasted_iota(jnp.int32, (SLOTS, GATHER_TILE), 1)
    sub = lax.broadcasted_iota(jnp.int32, (SUBLANES, LANES), 0)

    def fold(acc, parts, t):
        return jnp.where(lane == t, jnp.sum(parts, axis=1, keepdims=True), acc)

    def partials(t):
        h_lo = jnp.concatenate([h_ref[t, 0]] * 2, axis=0)
        h_hi = jnp.concatenate([h_ref[t, 1]] * 2, axis=0)
        cols = []
        for m in range(SLOTS // SUBLANES):
            ps = []
            for q in range(ROW_SUBLANES):
                lo, hi = _unpack(_pair_rows(tab_ref, idx_refs, t, SUBLANES * m + q,
                                            SUBLANES * m + ROW_SUBLANES + q))
                ps.append(lo * h_lo + hi * h_hi)
            cols.append(_half_sums(*ps, sub))
        return jnp.concatenate(cols, axis=0)

    def token(t, carry):
        return fold(carry[0], carry[1], t - 1), partials(t)

    acc, parts = lax.fori_loop(1, GATHER_TILE, token,
                               (jnp.zeros((SLOTS, GATHER_TILE), F32), partials(0)))
    act = fold(acc, parts, GATHER_TILE - 1)
    w_ref[0] = gate_ref[0] * _gelu(act)


def _slot_major_spec():
    per = ROUTE_TILE // GATHER_TILE
    return pl.BlockSpec((1, SLOTS, GATHER_TILE), lambda i: (i // per, 0, i % per))


def _smem_idx_specs():
    spec = pl.BlockSpec((GATHER_TILE * IDX_GROUP,), lambda i: (i,), memory_space=pltpu.SMEM,
                        pipeline_mode=pl.Buffered(1))
    return [spec] * IDX_ARRAYS


def _acts(idx, h2r, gates, table):
    n = h2r.shape[0]
    return pl.pallas_call(
        _acts_kernel,
        grid=(n // GATHER_TILE,),
        in_specs=_smem_idx_specs() + [
            pl.BlockSpec((GATHER_TILE, 2, ROW_SUBLANES, LANES), lambda i: (i, 0, 0, 0)),
            _slot_major_spec(),
            pl.BlockSpec(memory_space=pltpu.VMEM),
        ],
        out_specs=_slot_major_spec(),
        out_shape=jax.ShapeDtypeStruct(gates.shape, F32),
        compiler_params=pltpu.CompilerParams(dimension_semantics=("arbitrary",),
                                             vmem_limit_bytes=VMEM_LIMIT),
        name="acts",
    )(*idx, h2r, gates, table)


def _combine_kernel(*refs):
    idx_refs, (w_ref, tab_ref, out_ref, wl_buf) = refs[:IDX_ARRAYS], refs[IDX_ARRAYS:]
    chains = 2
    low_rows = lax.broadcasted_iota(jnp.int32, (SUBLANES, LANES), 0) < ROW_SUBLANES
    lane = lax.broadcasted_iota(jnp.int32, (SLOTS, GATHER_TILE), 1)

    def weights(t):
        return jnp.sum(jnp.where(lane == t, w_ref[0], 0.0), axis=1, keepdims=True)

    def token(t, col):
        wl_buf[...] = jnp.broadcast_to(col, (SLOTS, LANES))
        nxt = weights(t + 1)
        acc_lo = [jnp.zeros((SUBLANES, LANES), F32) for _ in range(chains)]
        acc_hi = [jnp.zeros((SUBLANES, LANES), F32) for _ in range(chains)]
        for k in range(SLOTS // 2):
            ja, jb = 2 * k, 2 * k + 1
            lo, hi = _unpack(_pair_rows(tab_ref, idx_refs, t, ja, jb))
            w = jnp.where(low_rows, jnp.broadcast_to(wl_buf[ja:ja + 1, :], (SUBLANES, LANES)),
                          jnp.broadcast_to(wl_buf[jb:jb + 1, :], (SUBLANES, LANES)))
            acc_lo[k % chains] = acc_lo[k % chains] + w * lo
            acc_hi[k % chains] = acc_hi[k % chains] + w * hi
        lo = acc_lo[0] + acc_lo[1]
        hi = acc_hi[0] + acc_hi[1]
        out_ref[t, 0:ROW_SUBLANES, :] = lo[0:ROW_SUBLANES] + lo[ROW_SUBLANES:SUBLANES]
        out_ref[t, ROW_SUBLANES:SUBLANES, :] = hi[0:ROW_SUBLANES] + hi[ROW_SUBLANES:SUBLANES]
        return nxt

    lax.fori_loop(0, GATHER_TILE, token, weights(0))


def _combine(idx, w, table):
    n = w.shape[0] * w.shape[2]
    return pl.pallas_call(
        _combine_kernel,
        grid=(n // GATHER_TILE,),
        in_specs=_smem_idx_specs() + [_slot_major_spec(), pl.BlockSpec(memory_space=pltpu.VMEM)],
        out_specs=pl.BlockSpec((GATHER_TILE, 2 * ROW_SUBLANES, LANES), lambda i: (i, 0, 0)),
        out_shape=jax.ShapeDtypeStruct((n, 2 * ROW_SUBLANES, LANES), F32),
        scratch_shapes=[pltpu.VMEM((SLOTS, LANES), F32)],
        compiler_params=pltpu.CompilerParams(dimension_semantics=("arbitrary",),
                                             vmem_limit_bytes=VMEM_LIMIT),
        name="combine",
    )(*idx, w, table)


def _final_kernel(x1_ref, peer_ref, g_ref, out_ref):
    out_ref[...] = _rms(x1_ref[...] + peer_ref[...], g_ref[...])


def _final(x1, peer, g_final):
    n = x1.shape[0]
    spec = pl.BlockSpec((FINAL_TILE, D_MODEL), lambda i: (i, 0))
    return pl.pallas_call(
        _final_kernel,
        grid=(n // FINAL_TILE,),
        in_specs=[spec, spec, pl.BlockSpec((1, D_MODEL), lambda i: (0, 0))],
        out_specs=spec,
        out_shape=jax.ShapeDtypeStruct((n, D_MODEL), F32),
        compiler_params=pltpu.CompilerParams(dimension_semantics=("arbitrary",),
                                             vmem_limit_bytes=VMEM_LIMIT),
        name="final",
    )(x1, peer, g_final)


def _index_arrays(idx):
    nb, _, tile = idx.shape
    off = (idx * ROW_SUBLANES).reshape(nb, IDX_ARRAYS, IDX_GROUP, tile)
    off = off.transpose(1, 0, 3, 2).reshape(IDX_ARRAYS, nb * tile * IDX_GROUP)
    return [off[a] for a in range(IDX_ARRAYS)]


def kernel(x, g_mix, w_in, pool_group_w, pool_scale, conv_w, w_branch_pool, w_branch_conv, w_out, g_ffn,
           w_q, sub_keys, expert_u, expert_v, g_final):
    batch, seq, d = x.shape
    n = batch * seq
    assert g_mix.shape[0] == 1, "the final residual add is fused with the final norm: one layer only"
    l = 0
    x1 = _mixer(x.reshape(n, d), g_mix[l][None, :], w_in[l].astype(BF16), pool_group_w[l].astype(BF16),
                pool_scale[l][None, :], conv_w[l], w_branch_pool[l].astype(BF16),
                w_branch_conv[l].astype(BF16), w_out[l].astype(BF16), batch, seq)
    h2, idx, gates = _route(x1, g_ffn[l][None, :], w_q[l].astype(BF16), sub_keys[l].astype(BF16))
    idx = _index_arrays(idx)
    h2r = h2.reshape(n, 2, ROW_SUBLANES, LANES)
    wts = _acts(idx, h2r, gates, _pack_table(expert_u[l]))
    peer = _combine(idx, wts, _pack_table(expert_v[l]))
    return _final(x1, peer.reshape(n, d), g_final[None, :]).reshape(batch, seq, d)
```

```python
import functools
import math

import jax
import jax.numpy as jnp
from jax import lax
from jax.experimental import pallas as pl
from jax.experimental.pallas import tpu as pltpu

D_MODEL = 1024
D_POOL = 512
POOL_WINDOWS = (2, 4, 8, 16)
POOL_GROUP_DIM = 128
D_CONV = 512
CONV_WIDTH = 3
N_KEYS = 128
N_EXPERTS = N_KEYS * N_KEYS
PEER_HEADS = 8
PEER_HALF = 128
PEER_TOPK = 16
SLOTS = PEER_HEADS * PEER_TOPK
RMS_EPS = 1e-6

LANES = 128
HALF_D = D_MODEL // 2
ROW_SUBLANES = HALF_D // LANES

MIX_TILE = 512
POOL_HIST = 16
CONV_HIST = 8
ROUTE_TILE = 256
GATHER_TILE = 128
FINAL_TILE = 1024
VMEM_LIMIT = 56 * 1024 * 1024

F32 = jnp.float32
BF16 = jnp.bfloat16


def _rms(x, g):
    return x * lax.rsqrt(jnp.mean(x * x, axis=-1, keepdims=True) + RMS_EPS) * g


def _dot(a, b):
    return jnp.dot(a, b, preferred_element_type=F32)


def _mixer_kernel(x_ref, g_ref, win_ref, pgw_ref, pscale_ref, convw_ref, wbp_ref, wbc_ref, wout_ref,
                  x1_ref, zp_buf, u_buf):
    s = pl.program_id(1)
    ts = MIX_TILE

    @pl.when(s == 0)
    def _():
        zp_buf[0:POOL_HIST, :] = jnp.zeros((POOL_HIST, D_POOL), F32)
        u_buf[0:CONV_HIST, :] = jnp.zeros((CONV_HIST, D_CONV), F32)

    x = x_ref[...]
    hb = _rms(x, g_ref[...]).astype(BF16)

    zp = _dot(hb, win_ref[:, 0:D_POOL])
    zp_buf[POOL_HIST:POOL_HIST + ts, :] = zp
    pos = s * ts + lax.broadcasted_iota(jnp.int32, (ts, 1), 0)
    outs = []
    for g, w in enumerate(POOL_WINDOWS):
        c0 = g * POOL_GROUP_DIM
        cur = zp[:, c0:c0 + POOL_GROUP_DIM]
        acc = cur
        for k in range(1, w):
            acc = acc + zp_buf[POOL_HIST - k:POOL_HIST - k + ts, c0:c0 + POOL_GROUP_DIM]
        cnt = jnp.minimum(pos + 1, w).astype(F32)
        pg = acc / cnt - cur
        og = _dot(pg.astype(BF16), pgw_ref[g]) * pscale_ref[:, c0:c0 + POOL_GROUP_DIM]
        outs.append(og)
    pm = jnp.concatenate(outs, axis=1)
    a = _dot(pm.astype(BF16), wbp_ref[...])
    zp_buf[0:POOL_HIST, :] = zp_buf[ts:ts + POOL_HIST, :]

    o = D_POOL
    zc = _dot(hb, win_ref[:, o:o + D_CONV])
    zb = _dot(hb, win_ref[:, o + D_CONV:o + 2 * D_CONV])
    zv = _dot(hb, win_ref[:, o + 2 * D_CONV:o + 3 * D_CONV])
    u = zc * zv
    u_buf[CONV_HIST:CONV_HIST + ts, :] = u
    y = (u_buf[CONV_HIST - 2:CONV_HIST - 2 + ts, :] * convw_ref[0:1, :]
         + u_buf[CONV_HIST - 1:CONV_HIST - 1 + ts, :] * convw_ref[1:2, :]
         + u * convw_ref[2:3, :])
    b = _dot((zb * y).astype(BF16), wbc_ref[...])
    u_buf[0:CONV_HIST, :] = u_buf[ts:ts + CONV_HIST, :]

    o = D_POOL + 3 * D_CONV
    ga = _dot(hb, win_ref[:, o:o + D_MODEL])
    gb = _dot(hb, win_ref[:, o + D_MODEL:o + 2 * D_MODEL])
    m = jax.nn.sigmoid(ga) * a + jax.nn.sigmoid(gb) * b
    x1_ref[...] = x + _dot(m.astype(BF16), wout_ref[...])


def _const_spec(shape):
    nd = len(shape)
    return pl.BlockSpec(shape, lambda *_: (0,) * nd, pipeline_mode=pl.Buffered(1))


def _mixer(x2d, g_mix, w_in, pgw, pscale, conv_w, wbp, wbc, w_out, batch, seq):
    n = batch * seq
    n_s = seq // MIX_TILE
    d_in = w_in.shape[1]
    return pl.pallas_call(
        _mixer_kernel,
        grid=(batch, n_s),
        in_specs=[
            pl.BlockSpec((MIX_TILE, D_MODEL), lambda b, s: (b * n_s + s, 0)),
            _const_spec((1, D_MODEL)),
            _const_spec((D_MODEL, d_in)),
            _const_spec((len(POOL_WINDOWS), POOL_GROUP_DIM, POOL_GROUP_DIM)),
            _const_spec((1, D_POOL)),
            _const_spec((CONV_WIDTH, D_CONV)),
            _const_spec((D_POOL, D_MODEL)),
            _const_spec((D_CONV, D_MODEL)),
            _const_spec((D_MODEL, D_MODEL)),
        ],
        out_specs=pl.BlockSpec((MIX_TILE, D_MODEL), lambda b, s: (b * n_s + s, 0)),
        out_shape=jax.ShapeDtypeStruct((n, D_MODEL), F32),
        scratch_shapes=[pltpu.VMEM((POOL_HIST + MIX_TILE, D_POOL), F32),
                        pltpu.VMEM((CONV_HIST + MIX_TILE, D_CONV), F32)],
        compiler_params=pltpu.CompilerParams(dimension_semantics=("arbitrary", "arbitrary"),
                                             vmem_limit_bytes=VMEM_LIMIT),
        name="mixer",
    )(x2d, g_mix, w_in, pgw, pscale, conv_w, wbp, wbc, w_out)


def _topk_axis0(s, k, pos, payload=None):
    vals, sel = [], []
    for _ in range(k):
        m = jnp.max(s, axis=0, keepdims=True)
        p = jnp.min(jnp.where(s == m, pos, jnp.inf), axis=0, keepdims=True)
        hit = pos == p
        vals.append(m)
        if payload is None:
            sel.append(p)
        else:
            sel.append(jnp.max(jnp.where(hit, payload, -1.0), axis=0, keepdims=True))
        s = jnp.where(hit, -jnp.inf, s)
    return jnp.concatenate(vals, axis=0), jnp.concatenate(sel, axis=0)


_CAND_GROUPS = ((range(0, 1), range(0, 16)),) + tuple((range(i, i + 1), range(0, 8)) for i in range(1, 8)) \
    + ((range(8, 16), range(0, 1)),)


def _candidates(top_s, top_i):
    t = top_s[0].shape[1]
    cs, cp, ci = [], [], []
    for i_rng, j_rng in _CAND_GROUPS:
        i0, i1, j0, j1 = i_rng.start, i_rng.stop, j_rng.start, j_rng.stop
        rows = max(i1 - i0, j1 - j0)
        r = lax.broadcasted_iota(jnp.int32, (rows, t), 0).astype(F32)
        cs.append(top_s[0][i0:i1, :] + top_s[1][j0:j1, :])
        ci.append(top_i[0][i0:i1, :] * N_KEYS + top_i[1][j0:j1, :])
        cp.append(r * PEER_TOPK + i0 * PEER_TOPK + j0 if i1 - i0 > 1 else r + i0 * PEER_TOPK + j0)
    return jnp.concatenate(cs, axis=0), jnp.concatenate(cp, axis=0), jnp.concatenate(ci, axis=0)


def _route_kernel(x1_ref, g_ref, wq_ref, keys_ref, h2_ref, idx_ref, gate_ref, hb_buf):
    h = pl.program_id(1)

    @pl.when(h == 0)
    def _():
        h2 = _rms(x1_ref[...], g_ref[...])
        h2_ref[...] = h2
        hb_buf[...] = h2.astype(BF16)

    q = _dot(hb_buf[...], wq_ref[...]).astype(BF16)
    for c in range(ROUTE_TILE // LANES):
        tok = slice(c * LANES, (c + 1) * LANES)
        key_pos = lax.broadcasted_iota(jnp.int32, (N_KEYS, LANES), 0).astype(F32)
        top_s, top_i = [], []
        for p in range(2):
            qs = q[tok, p * PEER_HALF:(p + 1) * PEER_HALF]
            st = lax.dot_general(keys_ref[0, p], qs, (((1,), (1,)), ((), ())),
                                 preferred_element_type=F32)
            v, i = _topk_axis0(st, PEER_TOPK, key_pos)
            top_s.append(v)
            top_i.append(i)
        cand_s, cand_pos, cand_i = _candidates(top_s, top_i)
        fin_s, fin_i = _topk_axis0(cand_s, PEER_TOPK, cand_pos, payload=cand_i)
        e = jnp.exp(fin_s - fin_s[0:1, :])
        gates = e / jnp.sum(e, axis=0, keepdims=True)
        idx_ref[0, :, tok] = (fin_i * ROW_SUBLANES).astype(jnp.int32)
        gate_ref[0, :, tok] = gates


def _route(x1, g_ffn, w_q, keys):
    n = x1.shape[0]
    nb = n // ROUTE_TILE
    return pl.pallas_call(
        _route_kernel,
        grid=(nb, PEER_HEADS),
        in_specs=[
            pl.BlockSpec((ROUTE_TILE, D_MODEL), lambda i, h: (i, 0)),
            pl.BlockSpec((1, D_MODEL), lambda i, h: (0, 0)),
            pl.BlockSpec((D_MODEL, 2 * PEER_HALF), lambda i, h: (0, h)),
            pl.BlockSpec((1, 2, N_KEYS, PEER_HALF), lambda i, h: (h, 0, 0, 0)),
        ],
        out_specs=[
            pl.BlockSpec((ROUTE_TILE, D_MODEL), lambda i, h: (i, 0)),
            pl.BlockSpec((1, PEER_TOPK, ROUTE_TILE), lambda i, h: (i, h, 0)),
            pl.BlockSpec((1, PEER_TOPK, ROUTE_TILE), lambda i, h: (i, h, 0)),
        ],
        out_shape=[
            jax.ShapeDtypeStruct((n, D_MODEL), F32),
            jax.ShapeDtypeStruct((nb, SLOTS, ROUTE_TILE), jnp.int32),
            jax.ShapeDtypeStruct((nb, SLOTS, ROUTE_TILE), F32),
        ],
        scratch_shapes=[pltpu.VMEM((ROUTE_TILE, D_MODEL), BF16)],
        compiler_params=pltpu.CompilerParams(dimension_semantics=("arbitrary", "arbitrary"),
                                             vmem_limit_bytes=VMEM_LIMIT),
        name="route",
    )(x1, g_ffn, w_q, keys)


def _pack_table(w):
    bits = lax.bitcast_convert_type(w.astype(BF16), jnp.uint16).astype(jnp.uint32)
    word = bits[:, :HALF_D] | (bits[:, HALF_D:] << 16)
    return word.reshape(w.shape[0] * ROW_SUBLANES, LANES)


def _unpack(row):
    lo = lax.bitcast_convert_type(row << 16, F32)
    hi = lax.bitcast_convert_type(row & jnp.uint32(0xFFFF0000), F32)
    return lo, hi


def _gelu(x):
    return 0.5 * x * (1.0 + lax.erf(x * (1.0 / math.sqrt(2.0))))


SUBLANES = 2 * ROW_SUBLANES


IDX_GROUP = 8
IDX_ARRAYS = SLOTS // IDX_GROUP


def _pair_rows(tab_ref, idx_refs, t, ja, jb):
    def row(j):
        off = idx_refs[j // IDX_GROUP][j % IDX_GROUP, t]
        return tab_ref[pl.ds(pl.multiple_of(off, ROW_SUBLANES), ROW_SUBLANES), :]
    return jnp.concatenate([row(ja), row(jb)], axis=0)


def _half_sums(p0, p1, p2, p3, sub):
    in_lo2 = (sub & 3) < 2
    even = (sub & 1) == 0

    def fold2(a, b):
        return (jnp.where(in_lo2, a, pltpu.roll(b, 2, 0))
                + jnp.where(in_lo2, pltpu.roll(a, SUBLANES - 2, 0), b))

    y02, y13 = fold2(p0, p2), fold2(p1, p3)
    return (jnp.where(even, y02, pltpu.roll(y13, 1, 0))
            + jnp.where(even, pltpu.roll(y02, SUBLANES - 1, 0), y13))


def _acts_kernel(*refs):
    idx_refs, (h_ref, gate_ref, tab_ref, w_ref) = refs[:IDX_ARRAYS], refs[IDX_ARRAYS:]
    lane = lax.broadcasted_iota(jnp.int32, (SLOTS, GATHER_TILE), 1)
    sub = lax.broadcasted_iota(jnp.int32, (SUBLANES, LANES), 0)

    def fold(acc, parts, t):
        return jnp.where(lane == t, jnp.sum(parts, axis=1, keepdims=True), acc)

    def partials(t):
        h_lo = jnp.concatenate([h_ref[t, 0]] * 2, axis=0)
        h_hi = jnp.concatenate([h_ref[t, 1]] * 2, axis=0)
        cols = []
        for m in range(SLOTS // SUBLANES):
            ps = []
            for q in range(ROW_SUBLANES):
                lo, hi = _unpack(_pair_rows(tab_ref, idx_refs, t, SUBLANES * m + q,
                                            SUBLANES * m + ROW_SUBLANES + q))
                ps.append(lo * h_lo + hi * h_hi)
            cols.append(_half_sums(*ps, sub))
        return jnp.concatenate(cols, axis=0)

    def token(t, carry):
        return fold(carry[0], carry[1], t - 1), partials(t)

    acc, parts = lax.fori_loop(1, GATHER_TILE, token,
                               (jnp.zeros((SLOTS, GATHER_TILE), F32), partials(0)))
    act = fold(acc, parts, GATHER_TILE - 1)
    w_ref[0] = gate_ref[0] * _gelu(act)


def _slot_major_spec():
    per = ROUTE_TILE // GATHER_TILE
    return pl.BlockSpec((1, SLOTS, GATHER_TILE), lambda i: (i // per, 0, i % per))


def _smem_idx_specs():
    per = ROUTE_TILE // GATHER_TILE
    return [pl.BlockSpec((None, IDX_GROUP, GATHER_TILE), functools.partial(lambda a, i: (i // per, a, i % per), a),
                         memory_space=pltpu.SMEM, pipeline_mode=pl.Buffered(1))
            for a in range(IDX_ARRAYS)]


def _acts(idx, h2r, gates, table):
    n = h2r.shape[0]
    return pl.pallas_call(
        _acts_kernel,
        grid=(n // GATHER_TILE,),
        in_specs=_smem_idx_specs() + [
            pl.BlockSpec((GATHER_TILE, 2, ROW_SUBLANES, LANES), lambda i: (i, 0, 0, 0)),
            _slot_major_spec(),
            pl.BlockSpec(memory_space=pltpu.VMEM),
        ],
        out_specs=_slot_major_spec(),
        out_shape=jax.ShapeDtypeStruct(gates.shape, F32),
        compiler_params=pltpu.CompilerParams(dimension_semantics=("arbitrary",),
                                             vmem_limit_bytes=VMEM_LIMIT),
        name="acts",
    )(*[idx] * IDX_ARRAYS, h2r, gates, table)


def _combine_kernel(*refs):
    idx_refs, (w_ref, tab_ref, out_ref, wl_buf) = refs[:IDX_ARRAYS], refs[IDX_ARRAYS:]
    chains = 2
    low_rows = lax.broadcasted_iota(jnp.int32, (SUBLANES, LANES), 0) < ROW_SUBLANES
    lane = lax.broadcasted_iota(jnp.int32, (SLOTS, GATHER_TILE), 1)

    def weights(t):
        return jnp.sum(jnp.where(lane == t, w_ref[0], 0.0), axis=1, keepdims=True)

    def token(t, col):
        wl_buf[...] = jnp.broadcast_to(col, (SLOTS, LANES))
        nxt = weights(t + 1)
        acc_lo = [jnp.zeros((SUBLANES, LANES), F32) for _ in range(chains)]
        acc_hi = [jnp.zeros((SUBLANES, LANES), F32) for _ in range(chains)]
        for k in range(SLOTS // 2):
            ja, jb = 2 * k, 2 * k + 1
            lo, hi = _unpack(_pair_rows(tab_ref, idx_refs, t, ja, jb))
            w = jnp.where(low_rows, jnp.broadcast_to(wl_buf[ja:ja + 1, :], (SUBLANES, LANES)),
                          jnp.broadcast_to(wl_buf[jb:jb + 1, :], (SUBLANES, LANES)))
            acc_lo[k % chains] = acc_lo[k % chains] + w * lo
            acc_hi[k % chains] = acc_hi[k % chains] + w * hi
        lo = acc_lo[0] + acc_lo[1]
        hi = acc_hi[0] + acc_hi[1]
        out_ref[t, 0:ROW_SUBLANES, :] = lo[0:ROW_SUBLANES] + lo[ROW_SUBLANES:SUBLANES]
        out_ref[t, ROW_SUBLANES:SUBLANES, :] = hi[0:ROW_SUBLANES] + hi[ROW_SUBLANES:SUBLANES]
        return nxt

    lax.fori_loop(0, GATHER_TILE, token, weights(0))


def _combine(idx, w, table):
    n = w.shape[0] * w.shape[2]
    return pl.pallas_call(
        _combine_kernel,
        grid=(n // GATHER_TILE,),
        in_specs=_smem_idx_specs() + [_slot_major_spec(), pl.BlockSpec(memory_space=pltpu.VMEM)],
        out_specs=pl.BlockSpec((GATHER_TILE, 2 * ROW_SUBLANES, LANES), lambda i: (i, 0, 0)),
        out_shape=jax.ShapeDtypeStruct((n, 2 * ROW_SUBLANES, LANES), F32),
        scratch_shapes=[pltpu.VMEM((SLOTS, LANES), F32)],
        compiler_params=pltpu.CompilerParams(dimension_semantics=("arbitrary",),
                                             vmem_limit_bytes=VMEM_LIMIT),
        name="combine",
    )(*[idx] * IDX_ARRAYS, w, table)


def _final_kernel(x1_ref, peer_ref, g_ref, out_ref):
    out_ref[...] = _rms(x1_ref[...] + peer_ref[...], g_ref[...])


def _final(x1, peer, g_final):
    n = x1.shape[0]
    spec = pl.BlockSpec((FINAL_TILE, D_MODEL), lambda i: (i, 0))
    return pl.pallas_call(
        _final_kernel,
        grid=(n // FINAL_TILE,),
        in_specs=[spec, spec, pl.BlockSpec((1, D_MODEL), lambda i: (0, 0))],
        out_specs=spec,
        out_shape=jax.ShapeDtypeStruct((n, D_MODEL), F32),
        compiler_params=pltpu.CompilerParams(dimension_semantics=("arbitrary",),
                                             vmem_limit_bytes=VMEM_LIMIT),
        name="final",
    )(x1, peer, g_final)


def kernel(x, g_mix, w_in, pool_group_w, pool_scale, conv_w, w_branch_pool, w_branch_conv, w_out, g_ffn,
           w_q, sub_keys, expert_u, expert_v, g_final):
    batch, seq, d = x.shape
    n = batch * seq
    assert g_mix.shape[0] == 1, "the final residual add is fused with the final norm: one layer only"
    l = 0
    x1 = _mixer(x.reshape(n, d), g_mix[l][None, :], w_in[l].astype(BF16), pool_group_w[l].astype(BF16),
                pool_scale[l][None, :], conv_w[l], w_branch_pool[l].astype(BF16),
                w_branch_conv[l].astype(BF16), w_out[l].astype(BF16), batch, seq)
    h2, idx, gates = _route(x1, g_ffn[l][None, :], w_q[l].astype(BF16), sub_keys[l].astype(BF16))
    h2r = h2.reshape(n, 2, ROW_SUBLANES, LANES)
    wts = _acts(idx, h2r, gates, _pack_table(expert_u[l]))
    peer = _combine(idx, wts, _pack_table(expert_v[l]))
    return _final(x1, peer.reshape(n, d), g_final[None, :]).reshape(batch, seq, d)
```

```python
import functools
import math

import jax
import jax.numpy as jnp
from jax import lax
from jax.experimental import pallas as pl
from jax.experimental.pallas import tpu as pltpu

D_MODEL = 1024
D_POOL = 512
POOL_WINDOWS = (2, 4, 8, 16)
POOL_GROUP_DIM = 128
D_CONV = 512
CONV_WIDTH = 3
N_KEYS = 128
N_EXPERTS = N_KEYS * N_KEYS
PEER_HEADS = 8
PEER_HALF = 128
PEER_TOPK = 16
SLOTS = PEER_HEADS * PEER_TOPK
RMS_EPS = 1e-6

LANES = 128
HALF_D = D_MODEL // 2
ROW_SUBLANES = HALF_D // LANES

MIX_TILE = 512
POOL_HIST = 16
CONV_HIST = 8
ROUTE_TILE = 256
GATHER_TILE = 128
FINAL_TILE = 1024
VMEM_LIMIT = 56 * 1024 * 1024

F32 = jnp.float32
BF16 = jnp.bfloat16


def _rms(x, g):
    return x * lax.rsqrt(jnp.mean(x * x, axis=-1, keepdims=True) + RMS_EPS) * g


def _dot(a, b):
    return jnp.dot(a, b, preferred_element_type=F32)


def _mixer_kernel(x_ref, g_ref, win_ref, pgw_ref, pscale_ref, convw_ref, wbp_ref, wbc_ref, wout_ref,
                  x1_ref, zp_buf, u_buf):
    s = pl.program_id(1)
    ts = MIX_TILE

    @pl.when(s == 0)
    def _():
        zp_buf[0:POOL_HIST, :] = jnp.zeros((POOL_HIST, D_POOL), F32)
        u_buf[0:CONV_HIST, :] = jnp.zeros((CONV_HIST, D_CONV), F32)

    x = x_ref[...]
    hb = _rms(x, g_ref[...]).astype(BF16)

    zp = _dot(hb, win_ref[:, 0:D_POOL])
    zp_buf[POOL_HIST:POOL_HIST + ts, :] = zp
    pos = s * ts + lax.broadcasted_iota(jnp.int32, (ts, 1), 0)
    outs = []
    for g, w in enumerate(POOL_WINDOWS):
        c0 = g * POOL_GROUP_DIM
        cur = zp[:, c0:c0 + POOL_GROUP_DIM]
        acc = cur
        for k in range(1, w):
            acc = acc + zp_buf[POOL_HIST - k:POOL_HIST - k + ts, c0:c0 + POOL_GROUP_DIM]
        cnt = jnp.minimum(pos + 1, w).astype(F32)
        pg = acc / cnt - cur
        og = _dot(pg.astype(BF16), pgw_ref[g]) * pscale_ref[:, c0:c0 + POOL_GROUP_DIM]
        outs.append(og)
    pm = jnp.concatenate(outs, axis=1)
    a = _dot(pm.astype(BF16), wbp_ref[...])
    zp_buf[0:POOL_HIST, :] = zp_buf[ts:ts + POOL_HIST, :]

    o = D_POOL
    zc = _dot(hb, win_ref[:, o:o + D_CONV])
    zb = _dot(hb, win_ref[:, o + D_CONV:o + 2 * D_CONV])
    zv = _dot(hb, win_ref[:, o + 2 * D_CONV:o + 3 * D_CONV])
    u = zc * zv
    u_buf[CONV_HIST:CONV_HIST + ts, :] = u
    y = (u_buf[CONV_HIST - 2:CONV_HIST - 2 + ts, :] * convw_ref[0:1, :]
         + u_buf[CONV_HIST - 1:CONV_HIST - 1 + ts, :] * convw_ref[1:2, :]
         + u * convw_ref[2:3, :])
    b = _dot((zb * y).astype(BF16), wbc_ref[...])
    u_buf[0:CONV_HIST, :] = u_buf[ts:ts + CONV_HIST, :]

    o = D_POOL + 3 * D_CONV
    ga = _dot(hb, win_ref[:, o:o + D_MODEL])
    gb = _dot(hb, win_ref[:, o + D_MODEL:o + 2 * D_MODEL])
    m = jax.nn.sigmoid(ga) * a + jax.nn.sigmoid(gb) * b
    x1_ref[...] = x + _dot(m.astype(BF16), wout_ref[...])


def _const_spec(shape):
    nd = len(shape)
    return pl.BlockSpec(shape, lambda *_: (0,) * nd, pipeline_mode=pl.Buffered(1))


def _mixer(x2d, g_mix, w_in, pgw, pscale, conv_w, wbp, wbc, w_out, batch, seq):
    n = batch * seq
    n_s = seq // MIX_TILE
    d_in = w_in.shape[1]
    return pl.pallas_call(
        _mixer_kernel,
        grid=(batch, n_s),
        in_specs=[
            pl.BlockSpec((MIX_TILE, D_MODEL), lambda b, s: (b * n_s + s, 0)),
            _const_spec((1, D_MODEL)),
            _const_spec((D_MODEL, d_in)),
            _const_spec((len(POOL_WINDOWS), POOL_GROUP_DIM, POOL_GROUP_DIM)),
            _const_spec((1, D_POOL)),
            _const_spec((CONV_WIDTH, D_CONV)),
            _const_spec((D_POOL, D_MODEL)),
            _const_spec((D_CONV, D_MODEL)),
            _const_spec((D_MODEL, D_MODEL)),
        ],
        out_specs=pl.BlockSpec((MIX_TILE, D_MODEL), lambda b, s: (b * n_s + s, 0)),
        out_shape=jax.ShapeDtypeStruct((n, D_MODEL), F32),
        scratch_shapes=[pltpu.VMEM((POOL_HIST + MIX_TILE, D_POOL), F32),
                        pltpu.VMEM((CONV_HIST + MIX_TILE, D_CONV), F32)],
        compiler_params=pltpu.CompilerParams(dimension_semantics=("arbitrary", "arbitrary"),
                                             vmem_limit_bytes=VMEM_LIMIT),
        name="mixer",
    )(x2d, g_mix, w_in, pgw, pscale, conv_w, wbp, wbc, w_out)


def _topk_axis0(s, k, pos, payload=None):
    vals, sel = [], []
    for _ in range(k):
        m = jnp.max(s, axis=0, keepdims=True)
        p = jnp.min(jnp.where(s == m, pos, jnp.inf), axis=0, keepdims=True)
        hit = pos == p
        vals.append(m)
        if payload is None:
            sel.append(p)
        else:
            sel.append(jnp.max(jnp.where(hit, payload, -1.0), axis=0, keepdims=True))
        s = jnp.where(hit, -jnp.inf, s)
    return jnp.concatenate(vals, axis=0), jnp.concatenate(sel, axis=0)


_CAND_GROUPS = ((range(0, 1), range(0, 16)),) + tuple((range(i, i + 1), range(0, 8)) for i in range(1, 8)) \
    + ((range(8, 16), range(0, 1)),)


def _candidates(top_s, top_i):
    t = top_s[0].shape[1]
    cs, cp, ci = [], [], []
    for i_rng, j_rng in _CAND_GROUPS:
        i0, i1, j0, j1 = i_rng.start, i_rng.stop, j_rng.start, j_rng.stop
        rows = max(i1 - i0, j1 - j0)
        r = lax.broadcasted_iota(jnp.int32, (rows, t), 0).astype(F32)
        cs.append(top_s[0][i0:i1, :] + top_s[1][j0:j1, :])
        ci.append(top_i[0][i0:i1, :] * N_KEYS + top_i[1][j0:j1, :])
        cp.append(r * PEER_TOPK + i0 * PEER_TOPK + j0 if i1 - i0 > 1 else r + i0 * PEER_TOPK + j0)
    return jnp.concatenate(cs, axis=0), jnp.concatenate(cp, axis=0), jnp.concatenate(ci, axis=0)


def _route_kernel(x1_ref, g_ref, wq_ref, keys_ref, h2_ref, idx_ref, gate_ref, hb_buf):
    h = pl.program_id(1)

    @pl.when(h == 0)
    def _():
        h2 = _rms(x1_ref[...], g_ref[...])
        h2_ref[...] = h2
        hb_buf[...] = h2.astype(BF16)

    q = _dot(hb_buf[...], wq_ref[...]).astype(BF16)
    for c in range(ROUTE_TILE // LANES):
        tok = slice(c * LANES, (c + 1) * LANES)
        key_pos = lax.broadcasted_iota(jnp.int32, (N_KEYS, LANES), 0).astype(F32)
        top_s, top_i = [], []
        for p in range(2):
            qs = q[tok, p * PEER_HALF:(p + 1) * PEER_HALF]
            st = lax.dot_general(keys_ref[0, p], qs, (((1,), (1,)), ((), ())),
                                 preferred_element_type=F32)
            v, i = _topk_axis0(st, PEER_TOPK, key_pos)
            top_s.append(v)
            top_i.append(i)
        cand_s, cand_pos, cand_i = _candidates(top_s, top_i)
        fin_s, fin_i = _topk_axis0(cand_s, PEER_TOPK, cand_pos, payload=cand_i)
        e = jnp.exp(fin_s - fin_s[0:1, :])
        gates = e / jnp.sum(e, axis=0, keepdims=True)
        idx_ref[0, :, tok] = (fin_i * ROW_SUBLANES).astype(jnp.int32)
        gate_ref[0, :, tok] = gates


def _route(x1, g_ffn, w_q, keys):
    n = x1.shape[0]
    nb = n // ROUTE_TILE
    return pl.pallas_call(
        _route_kernel,
        grid=(nb, PEER_HEADS),
        in_specs=[
            pl.BlockSpec((ROUTE_TILE, D_MODEL), lambda i, h: (i, 0)),
            pl.BlockSpec((1, D_MODEL), lambda i, h: (0, 0)),
            pl.BlockSpec((D_MODEL, 2 * PEER_HALF), lambda i, h: (0, h)),
            pl.BlockSpec((1, 2, N_KEYS, PEER_HALF), lambda i, h: (h, 0, 0, 0)),
        ],
        out_specs=[
            pl.BlockSpec((ROUTE_TILE, D_MODEL), lambda i, h: (i, 0)),
            pl.BlockSpec((1, PEER_TOPK, ROUTE_TILE), lambda i, h: (i, h, 0)),
            pl.BlockSpec((1, PEER_TOPK, ROUTE_TILE), lambda i, h: (i, h, 0)),
        ],
        out_shape=[
            jax.ShapeDtypeStruct((n, D_MODEL), F32),
            jax.ShapeDtypeStruct((nb, SLOTS, ROUTE_TILE), jnp.int32),
            jax.ShapeDtypeStruct((nb, SLOTS, ROUTE_TILE), F32),
        ],
        scratch_shapes=[pltpu.VMEM((ROUTE_TILE, D_MODEL), BF16)],
        compiler_params=pltpu.CompilerParams(dimension_semantics=("arbitrary", "arbitrary"),
                                             vmem_limit_bytes=VMEM_LIMIT),
        name="route",
    )(x1, g_ffn, w_q, keys)


def _pack_table(w):
    bits = lax.bitcast_convert_type(w.astype(BF16), jnp.uint16).astype(jnp.uint32)
    word = bits[:, :HALF_D] | (bits[:, HALF_D:] << 16)
    return word.reshape(w.shape[0] * ROW_SUBLANES, LANES)


def _gelu(x):
    return 0.5 * x * (1.0 + lax.erf(x * (1.0 / math.sqrt(2.0))))


SUBLANES = 2 * ROW_SUBLANES


IDX_GROUP = 8
IDX_ARRAYS = SLOTS // IDX_GROUP


def _pair_rows(tab_ref, idx_refs, t, ja, jb):
    def row(j):
        off = idx_refs[j // IDX_GROUP][j % IDX_GROUP, t]
        return tab_ref[pl.ds(pl.multiple_of(off, ROW_SUBLANES), ROW_SUBLANES), :]
    return jnp.concatenate([row(ja), row(jb)], axis=0)


PAIRS = SLOTS // 2
WORD_ROWS = PAIRS * SUBLANES
HALF_ROWS = 2 * WORD_ROWS


def _stacked_rows(tab_ref, idx_refs, t):
    z = jnp.concatenate([_pair_rows(tab_ref, idx_refs, t, 2 * k, 2 * k + 1) for k in range(PAIRS)], axis=0)
    return pltpu.bitcast(z, BF16)


def _slot_sum_matrix():
    c = jnp.arange(HALF_ROWS)
    return (c[:, None] // SUBLANES == jnp.arange(SLOTS)[None, :]).astype(BF16)


def _acts_kernel(*refs):
    idx_refs, (h_ref, gate_ref, ssum_ref, tab_ref, w_ref, dots_buf) = refs[:IDX_ARRAYS], refs[IDX_ARRAYS:]
    row = lax.broadcasted_iota(jnp.int32, (SUBLANES, HALF_ROWS), 0)
    col = lax.broadcasted_iota(jnp.int32, (SUBLANES, HALF_ROWS), 1)
    own = row == (col & (SUBLANES - 1))

    def token(t):
        d = lax.dot_general(h_ref[t].astype(BF16), _stacked_rows(tab_ref, idx_refs, t),
                            (((1,), (1,)), ((), ())), preferred_element_type=F32)
        return jnp.sum(jnp.where(own, d, 0.0), axis=0, keepdims=True)

    def group(g, carry):
        base = pl.multiple_of(g * SUBLANES, SUBLANES)
        dots_buf[pl.ds(base, SUBLANES), :] = jnp.concatenate([token(base + i) for i in range(SUBLANES)], axis=0)
        return carry

    lax.fori_loop(0, GATHER_TILE // SUBLANES, group, 0)
    v = dots_buf[...]
    hi = v.astype(BF16)
    r1 = v - hi.astype(F32)
    mid = r1.astype(BF16)
    lo = (r1 - mid.astype(F32)).astype(BF16)
    act = (_dot(hi, ssum_ref[...]) + _dot(mid, ssum_ref[...])) + _dot(lo, ssum_ref[...])
    w_ref[0] = gate_ref[0] * _gelu(act.T)


def _slot_major_spec():
    per = ROUTE_TILE // GATHER_TILE
    return pl.BlockSpec((1, SLOTS, GATHER_TILE), lambda i: (i // per, 0, i % per))


def _smem_idx_specs():
    per = ROUTE_TILE // GATHER_TILE
    return [pl.BlockSpec((None, IDX_GROUP, GATHER_TILE), functools.partial(lambda a, i: (i // per, a, i % per), a),
                         memory_space=pltpu.SMEM, pipeline_mode=pl.Buffered(1))
            for a in range(IDX_ARRAYS)]


def _acts(idx, h2r, gates, table):
    n = h2r.shape[0]
    return pl.pallas_call(
        _acts_kernel,
        grid=(n // GATHER_TILE,),
        in_specs=_smem_idx_specs() + [
            pl.BlockSpec((GATHER_TILE, SUBLANES, LANES), lambda i: (i, 0, 0)),
            _slot_major_spec(),
            _const_spec((HALF_ROWS, SLOTS)),
            pl.BlockSpec(memory_space=pltpu.VMEM),
        ],
        out_specs=_slot_major_spec(),
        out_shape=jax.ShapeDtypeStruct(gates.shape, F32),
        scratch_shapes=[pltpu.VMEM((GATHER_TILE, HALF_ROWS), F32)],
        compiler_params=pltpu.CompilerParams(dimension_semantics=("arbitrary",),
                                             vmem_limit_bytes=VMEM_LIMIT),
        name="acts",
    )(*[idx] * IDX_ARRAYS, h2r, gates, _slot_sum_matrix(), table)


def _expansion_matrix():
    return _slot_sum_matrix().T


def _combine_kernel(*refs):
    idx_refs, (w_ref, expand_ref, tab_ref, out_ref, wexp_buf) = refs[:IDX_ARRAYS], refs[IDX_ARRAYS:]
    row = lax.broadcasted_iota(jnp.int32, (SUBLANES, LANES), 0)
    col = lax.broadcasted_iota(jnp.int32, (SUBLANES, LANES), 1)
    own = row == ROW_SUBLANES * (col & 1) + ((col >> 1) & (ROW_SUBLANES - 1))
    wexp_buf[...] = lax.dot_general(w_ref[0].astype(BF16), expand_ref[...], (((0,), (0,)), ((), ())),
                                    preferred_element_type=F32)

    def token(t, wrow):
        zb = _stacked_rows(tab_ref, idx_refs, t)
        lhs = jnp.concatenate(
            [jnp.where(own, jnp.broadcast_to(wrow[:, i * LANES:(i + 1) * LANES], (SUBLANES, LANES)), 0.0)
             for i in range(HALF_ROWS // LANES)], axis=1).astype(BF16)
        out_ref[t] = jnp.dot(lhs, zb, preferred_element_type=F32)

    def group(g, carry):
        base = pl.multiple_of(g * SUBLANES, SUBLANES)
        w8 = wexp_buf[pl.ds(base, SUBLANES), :]
        for i in range(SUBLANES):
            token(base + i, w8[i:i + 1, :])
        return carry

    lax.fori_loop(0, GATHER_TILE // SUBLANES, group, 0)


def _combine(idx, w, table):
    n = w.shape[0] * w.shape[2]
    return pl.pallas_call(
        _combine_kernel,
        grid=(n // GATHER_TILE,),
        in_specs=_smem_idx_specs() + [_slot_major_spec(), _const_spec((SLOTS, HALF_ROWS)),
                                      pl.BlockSpec(memory_space=pltpu.VMEM)],
        out_specs=pl.BlockSpec((GATHER_TILE, 2 * ROW_SUBLANES, LANES), lambda i: (i, 0, 0)),
        out_shape=jax.ShapeDtypeStruct((n, 2 * ROW_SUBLANES, LANES), F32),
        scratch_shapes=[pltpu.VMEM((GATHER_TILE, HALF_ROWS), F32)],
        compiler_params=pltpu.CompilerParams(dimension_semantics=("arbitrary",),
                                             vmem_limit_bytes=VMEM_LIMIT),
        name="combine",
    )(*[idx] * IDX_ARRAYS, w, _expansion_matrix(), table)


def _final_kernel(x1_ref, peer_ref, g_ref, out_ref):
    out_ref[...] = _rms(x1_ref[...] + peer_ref[...], g_ref[...])


def _final(x1, peer, g_final):
    n = x1.shape[0]
    spec = pl.BlockSpec((FINAL_TILE, D_MODEL), lambda i: (i, 0))
    return pl.pallas_call(
        _final_kernel,
        grid=(n // FINAL_TILE,),
        in_specs=[spec, spec, pl.BlockSpec((1, D_MODEL), lambda i: (0, 0))],
        out_specs=spec,
        out_shape=jax.ShapeDtypeStruct((n, D_MODEL), F32),
        compiler_params=pltpu.CompilerParams(dimension_semantics=("arbitrary",),
                                             vmem_limit_bytes=VMEM_LIMIT),
        name="final",
    )(x1, peer, g_final)


def kernel(x, g_mix, w_in, pool_group_w, pool_scale, conv_w, w_branch_pool, w_branch_conv, w_out, g_ffn,
           w_q, sub_keys, expert_u, expert_v, g_final):
    batch, seq, d = x.shape
    n = batch * seq
    assert g_mix.shape[0] == 1, "the final residual add is fused with the final norm: one layer only"
    l = 0
    x1 = _mixer(x.reshape(n, d), g_mix[l][None, :], w_in[l].astype(BF16), pool_group_w[l].astype(BF16),
                pool_scale[l][None, :], conv_w[l], w_branch_pool[l].astype(BF16),
                w_branch_conv[l].astype(BF16), w_out[l].astype(BF16), batch, seq)
    h2, idx, gates = _route(x1, g_ffn[l][None, :], w_q[l].astype(BF16), sub_keys[l].astype(BF16))
    h2r = h2.reshape(n, 2, ROW_SUBLANES, LANES).transpose(0, 2, 1, 3).reshape(n, SUBLANES, LANES)
    wts = _acts(idx, h2r, gates, _pack_table(expert_u[l]))
    peer = _combine(idx, wts, _pack_table(expert_v[l]))
    return _final(x1, peer.reshape(n, d), g_final[None, :]).reshape(batch, seq, d)
```

```python
import functools
import math

import jax
import jax.numpy as jnp
from jax import lax
from jax.experimental import pallas as pl
from jax.experimental.pallas import tpu as pltpu

D_MODEL = 1024
D_POOL = 512
POOL_WINDOWS = (2, 4, 8, 16)
POOL_GROUP_DIM = 128
D_CONV = 512
CONV_WIDTH = 3
N_KEYS = 128
N_EXPERTS = N_KEYS * N_KEYS
PEER_HEADS = 8
PEER_HALF = 128
PEER_TOPK = 16
SLOTS = PEER_HEADS * PEER_TOPK
RMS_EPS = 1e-6

LANES = 128
HALF_D = D_MODEL // 2
ROW_SUBLANES = HALF_D // LANES

MIX_TILE = 512
POOL_HIST = 16
CONV_HIST = 8
ROUTE_TILE = 256
GATHER_TILE = 256
GROUP_TOKENS = 32
FINAL_TILE = 1024
VMEM_LIMIT = 56 * 1024 * 1024

F32 = jnp.float32
BF16 = jnp.bfloat16


def _rms(x, g):
    return x * lax.rsqrt(jnp.mean(x * x, axis=-1, keepdims=True) + RMS_EPS) * g


def _dot(a, b):
    return jnp.dot(a, b, preferred_element_type=F32)


def _mixer_kernel(x_ref, g_ref, win_ref, pgw_ref, pscale_ref, convw_ref, wbp_ref, wbc_ref, wout_ref,
                  x1_ref, zp_buf, u_buf):
    s = pl.program_id(1)
    ts = MIX_TILE

    @pl.when(s == 0)
    def _():
        zp_buf[0:POOL_HIST, :] = jnp.zeros((POOL_HIST, D_POOL), F32)
        u_buf[0:CONV_HIST, :] = jnp.zeros((CONV_HIST, D_CONV), F32)

    x = x_ref[...]
    hb = _rms(x, g_ref[...]).astype(BF16)

    zp = _dot(hb, win_ref[:, 0:D_POOL])
    zp_buf[POOL_HIST:POOL_HIST + ts, :] = zp
    pos = s * ts + lax.broadcasted_iota(jnp.int32, (ts, 1), 0)
    outs = []
    for g, w in enumerate(POOL_WINDOWS):
        c0 = g * POOL_GROUP_DIM
        cur = zp[:, c0:c0 + POOL_GROUP_DIM]
        acc = cur
        for k in range(1, w):
            acc = acc + zp_buf[POOL_HIST - k:POOL_HIST - k + ts, c0:c0 + POOL_GROUP_DIM]
        cnt = jnp.minimum(pos + 1, w).astype(F32)
        pg = acc / cnt - cur
        og = _dot(pg.astype(BF16), pgw_ref[g]) * pscale_ref[:, c0:c0 + POOL_GROUP_DIM]
        outs.append(og)
    pm = jnp.concatenate(outs, axis=1)
    a = _dot(pm.astype(BF16), wbp_ref[...])
    zp_buf[0:POOL_HIST, :] = zp_buf[ts:ts + POOL_HIST, :]

    o = D_POOL
    zc = _dot(hb, win_ref[:, o:o + D_CONV])
    zb = _dot(hb, win_ref[:, o + D_CONV:o + 2 * D_CONV])
    zv = _dot(hb, win_ref[:, o + 2 * D_CONV:o + 3 * D_CONV])
    u = zc * zv
    u_buf[CONV_HIST:CONV_HIST + ts, :] = u
    y = (u_buf[CONV_HIST - 2:CONV_HIST - 2 + ts, :] * convw_ref[0:1, :]
         + u_buf[CONV_HIST - 1:CONV_HIST - 1 + ts, :] * convw_ref[1:2, :]
         + u * convw_ref[2:3, :])
    b = _dot((zb * y).astype(BF16), wbc_ref[...])
    u_buf[0:CONV_HIST, :] = u_buf[ts:ts + CONV_HIST, :]

    o = D_POOL + 3 * D_CONV
    ga = _dot(hb, win_ref[:, o:o + D_MODEL])
    gb = _dot(hb, win_ref[:, o + D_MODEL:o + 2 * D_MODEL])
    m = jax.nn.sigmoid(ga) * a + jax.nn.sigmoid(gb) * b
    x1_ref[...] = x + _dot(m.astype(BF16), wout_ref[...])


def _const_spec(shape):
    nd = len(shape)
    return pl.BlockSpec(shape, lambda *_: (0,) * nd, pipeline_mode=pl.Buffered(1))


def _mixer(x2d, g_mix, w_in, pgw, pscale, conv_w, wbp, wbc, w_out, batch, seq):
    n = batch * seq
    n_s = seq // MIX_TILE
    d_in = w_in.shape[1]
    return pl.pallas_call(
        _mixer_kernel,
        grid=(batch, n_s),
        in_specs=[
            pl.BlockSpec((MIX_TILE, D_MODEL), lambda b, s: (b * n_s + s, 0)),
            _const_spec((1, D_MODEL)),
            _const_spec((D_MODEL, d_in)),
            _const_spec((len(POOL_WINDOWS), POOL_GROUP_DIM, POOL_GROUP_DIM)),
            _const_spec((1, D_POOL)),
            _const_spec((CONV_WIDTH, D_CONV)),
            _const_spec((D_POOL, D_MODEL)),
            _const_spec((D_CONV, D_MODEL)),
            _const_spec((D_MODEL, D_MODEL)),
        ],
        out_specs=pl.BlockSpec((MIX_TILE, D_MODEL), lambda b, s: (b * n_s + s, 0)),
        out_shape=jax.ShapeDtypeStruct((n, D_MODEL), F32),
        scratch_shapes=[pltpu.VMEM((POOL_HIST + MIX_TILE, D_POOL), F32),
                        pltpu.VMEM((CONV_HIST + MIX_TILE, D_CONV), F32)],
        compiler_params=pltpu.CompilerParams(dimension_semantics=("arbitrary", "arbitrary"),
                                             vmem_limit_bytes=VMEM_LIMIT),
        name="mixer",
    )(x2d, g_mix, w_in, pgw, pscale, conv_w, wbp, wbc, w_out)


def _topk_axis0(s, k, pos, payload=None):
    vals, sel = [], []
    for _ in range(k):
        m = jnp.max(s, axis=0, keepdims=True)
        p = jnp.min(jnp.where(s == m, pos, jnp.inf), axis=0, keepdims=True)
        hit = pos == p
        vals.append(m)
        if payload is None:
            sel.append(p)
        else:
            sel.append(jnp.max(jnp.where(hit, payload, -1.0), axis=0, keepdims=True))
        s = jnp.where(hit, -jnp.inf, s)
    return jnp.concatenate(vals, axis=0), jnp.concatenate(sel, axis=0)


_CAND_GROUPS = ((range(0, 1), range(0, 16)),) + tuple((range(i, i + 1), range(0, 8)) for i in range(1, 8)) \
    + ((range(8, 16), range(0, 1)),)


def _candidates(top_s, top_i):
    t = top_s[0].shape[1]
    cs, cp, ci = [], [], []
    for i_rng, j_rng in _CAND_GROUPS:
        i0, i1, j0, j1 = i_rng.start, i_rng.stop, j_rng.start, j_rng.stop
        rows = max(i1 - i0, j1 - j0)
        r = lax.broadcasted_iota(jnp.int32, (rows, t), 0).astype(F32)
        cs.append(top_s[0][i0:i1, :] + top_s[1][j0:j1, :])
        ci.append(top_i[0][i0:i1, :] * N_KEYS + top_i[1][j0:j1, :])
        cp.append(r * PEER_TOPK + i0 * PEER_TOPK + j0 if i1 - i0 > 1 else r + i0 * PEER_TOPK + j0)
    return jnp.concatenate(cs, axis=0), jnp.concatenate(cp, axis=0), jnp.concatenate(ci, axis=0)


def _route_kernel(x1_ref, g_ref, wq_ref, keys_ref, h2_ref, idx_ref, gate_ref, hb_buf):
    h = pl.program_id(1)

    @pl.when(h == 0)
    def _():
        h2 = _rms(x1_ref[...], g_ref[...])
        h2_ref[...] = h2
        hb_buf[...] = h2.astype(BF16)

    q = _dot(hb_buf[...], wq_ref[...]).astype(BF16)
    for c in range(ROUTE_TILE // LANES):
        tok = slice(c * LANES, (c + 1) * LANES)
        key_pos = lax.broadcasted_iota(jnp.int32, (N_KEYS, LANES), 0).astype(F32)
        top_s, top_i = [], []
        for p in range(2):
            qs = q[tok, p * PEER_HALF:(p + 1) * PEER_HALF]
            st = lax.dot_general(keys_ref[0, p], qs, (((1,), (1,)), ((), ())),
                                 preferred_element_type=F32)
            v, i = _topk_axis0(st, PEER_TOPK, key_pos)
            top_s.append(v)
            top_i.append(i)
        cand_s, cand_pos, cand_i = _candidates(top_s, top_i)
        fin_s, fin_i = _topk_axis0(cand_s, PEER_TOPK, cand_pos, payload=cand_i)
        e = jnp.exp(fin_s - fin_s[0:1, :])
        gates = e / jnp.sum(e, axis=0, keepdims=True)
        idx_ref[0, :, tok] = (fin_i * ROW_SUBLANES).astype(jnp.int32)
        gate_ref[0, :, tok] = gates


def _route(x1, g_ffn, w_q, keys):
    n = x1.shape[0]
    nb = n // ROUTE_TILE
    return pl.pallas_call(
        _route_kernel,
        grid=(nb, PEER_HEADS),
        in_specs=[
            pl.BlockSpec((ROUTE_TILE, D_MODEL), lambda i, h: (i, 0)),
            pl.BlockSpec((1, D_MODEL), lambda i, h: (0, 0)),
            pl.BlockSpec((D_MODEL, 2 * PEER_HALF), lambda i, h: (0, h)),
            pl.BlockSpec((1, 2, N_KEYS, PEER_HALF), lambda i, h: (h, 0, 0, 0)),
        ],
        out_specs=[
            pl.BlockSpec((ROUTE_TILE, D_MODEL), lambda i, h: (i, 0)),
            pl.BlockSpec((1, PEER_TOPK, ROUTE_TILE), lambda i, h: (i, h, 0)),
            pl.BlockSpec((1, PEER_TOPK, ROUTE_TILE), lambda i, h: (i, h, 0)),
        ],
        out_shape=[
            jax.ShapeDtypeStruct((n, D_MODEL), F32),
            jax.ShapeDtypeStruct((nb, SLOTS, ROUTE_TILE), jnp.int32),
            jax.ShapeDtypeStruct((nb, SLOTS, ROUTE_TILE), F32),
        ],
        scratch_shapes=[pltpu.VMEM((ROUTE_TILE, D_MODEL), BF16)],
        compiler_params=pltpu.CompilerParams(dimension_semantics=("arbitrary", "arbitrary"),
                                             vmem_limit_bytes=VMEM_LIMIT),
        name="route",
    )(x1, g_ffn, w_q, keys)


def _pack_table(w):
    bits = lax.bitcast_convert_type(w.astype(BF16), jnp.uint16).astype(jnp.uint32)
    word = bits[:, :HALF_D] | (bits[:, HALF_D:] << 16)
    return word.reshape(w.shape[0] * ROW_SUBLANES, LANES)


def _gelu(x):
    return 0.5 * x * (1.0 + lax.erf(x * (1.0 / math.sqrt(2.0))))


SUBLANES = 2 * ROW_SUBLANES


IDX_GROUP = 8
IDX_ARRAYS = SLOTS // IDX_GROUP


def _pair_rows(tab_ref, idx_refs, t, ja, jb):
    def row(j):
        off = idx_refs[j // IDX_GROUP][j % IDX_GROUP, t]
        return tab_ref[pl.ds(pl.multiple_of(off, ROW_SUBLANES), ROW_SUBLANES), :]
    return jnp.concatenate([row(ja), row(jb)], axis=0)


PAIRS = SLOTS // 2
WORD_ROWS = PAIRS * SUBLANES
HALF_ROWS = 2 * WORD_ROWS


def _stacked_rows(tab_ref, idx_refs, t):
    z = jnp.concatenate([_pair_rows(tab_ref, idx_refs, t, 2 * k, 2 * k + 1) for k in range(PAIRS)], axis=0)
    return pltpu.bitcast(z, BF16)


def _slot_sum_matrix():
    c = jnp.arange(HALF_ROWS)
    return (c[:, None] // SUBLANES == jnp.arange(SLOTS)[None, :]).astype(BF16)


def _acts_kernel(*refs):
    idx_refs, (h_ref, gate_ref, ssum_ref, tab_ref, w_ref, dots_buf) = refs[:IDX_ARRAYS], refs[IDX_ARRAYS:]
    row = lax.broadcasted_iota(jnp.int32, (SUBLANES, HALF_ROWS), 0)
    col = lax.broadcasted_iota(jnp.int32, (SUBLANES, HALF_ROWS), 1)
    own = row == (col & (SUBLANES - 1))

    def token(t):
        d = lax.dot_general(h_ref[t].astype(BF16), _stacked_rows(tab_ref, idx_refs, t),
                            (((1,), (1,)), ((), ())), preferred_element_type=F32)
        return jnp.sum(jnp.where(own, d, 0.0), axis=0, keepdims=True)

    def group(g, carry):
        base = pl.multiple_of(g * GROUP_TOKENS, GROUP_TOKENS)
        dots_buf[pl.ds(base, GROUP_TOKENS), :] = jnp.concatenate(
            [token(base + i) for i in range(GROUP_TOKENS)], axis=0)
        return carry

    lax.fori_loop(0, GATHER_TILE // GROUP_TOKENS, group, 0)
    v = dots_buf[...]
    hi = v.astype(BF16)
    r1 = v - hi.astype(F32)
    mid = r1.astype(BF16)
    lo = (r1 - mid.astype(F32)).astype(BF16)
    act = (_dot(hi, ssum_ref[...]) + _dot(mid, ssum_ref[...])) + _dot(lo, ssum_ref[...])
    w_ref[0] = gate_ref[0] * _gelu(act.T)


def _slot_major_spec():
    per = ROUTE_TILE // GATHER_TILE
    return pl.BlockSpec((1, SLOTS, GATHER_TILE), lambda i: (i // per, 0, i % per))


def _smem_idx_specs():
    per = ROUTE_TILE // GATHER_TILE
    return [pl.BlockSpec((None, IDX_GROUP, GATHER_TILE), functools.partial(lambda a, i: (i // per, a, i % per), a),
                         memory_space=pltpu.SMEM, pipeline_mode=pl.Buffered(1))
            for a in range(IDX_ARRAYS)]


def _acts(idx, h2r, gates, table):
    n = h2r.shape[0]
    return pl.pallas_call(
        _acts_kernel,
        grid=(n // GATHER_TILE,),
        in_specs=_smem_idx_specs() + [
            pl.BlockSpec((GATHER_TILE, SUBLANES, LANES), lambda i: (i, 0, 0)),
            _slot_major_spec(),
            _const_spec((HALF_ROWS, SLOTS)),
            pl.BlockSpec(memory_space=pltpu.VMEM),
        ],
        out_specs=_slot_major_spec(),
        out_shape=jax.ShapeDtypeStruct(gates.shape, F32),
        scratch_shapes=[pltpu.VMEM((GATHER_TILE, HALF_ROWS), F32)],
        compiler_params=pltpu.CompilerParams(dimension_semantics=("arbitrary",),
                                             vmem_limit_bytes=VMEM_LIMIT),
        name="acts",
    )(*[idx] * IDX_ARRAYS, h2r, gates, _slot_sum_matrix(), table)


def _expansion_matrix():
    return _slot_sum_matrix().T


def _combine_kernel(*refs):
    idx_refs, (w_ref, expand_ref, tab_ref, out_ref, wexp_buf) = refs[:IDX_ARRAYS], refs[IDX_ARRAYS:]
    row = lax.broadcasted_iota(jnp.int32, (SUBLANES, LANES), 0)
    col = lax.broadcasted_iota(jnp.int32, (SUBLANES, LANES), 1)
    own = row == ROW_SUBLANES * (col & 1) + ((col >> 1) & (ROW_SUBLANES - 1))
    wexp_buf[...] = lax.dot_general(w_ref[0].astype(BF16), expand_ref[...], (((0,), (0,)), ((), ())),
                                    preferred_element_type=F32)

    def token(t, wrow):
        zb = _stacked_rows(tab_ref, idx_refs, t)
        lhs = jnp.concatenate(
            [jnp.where(own, jnp.broadcast_to(wrow[:, i * LANES:(i + 1) * LANES], (SUBLANES, LANES)), 0.0)
             for i in range(HALF_ROWS // LANES)], axis=1).astype(BF16)
        out_ref[t] = jnp.dot(lhs, zb, preferred_element_type=F32)

    def group(g, carry):
        base = pl.multiple_of(g * GROUP_TOKENS, GROUP_TOKENS)
        wg = wexp_buf[pl.ds(base, GROUP_TOKENS), :]
        for i in range(GROUP_TOKENS):
            token(base + i, wg[i:i + 1, :])
        return carry

    lax.fori_loop(0, GATHER_TILE // GROUP_TOKENS, group, 0)


def _combine(idx, w, table):
    n = w.shape[0] * w.shape[2]
    return pl.pallas_call(
        _combine_kernel,
        grid=(n // GATHER_TILE,),
        in_specs=_smem_idx_specs() + [_slot_major_spec(), _const_spec((SLOTS, HALF_ROWS)),
                                      pl.BlockSpec(memory_space=pltpu.VMEM)],
        out_specs=pl.BlockSpec((GATHER_TILE, 2 * ROW_SUBLANES, LANES), lambda i: (i, 0, 0)),
        out_shape=jax.ShapeDtypeStruct((n, 2 * ROW_SUBLANES, LANES), F32),
        scratch_shapes=[pltpu.VMEM((GATHER_TILE, HALF_ROWS), F32)],
        compiler_params=pltpu.CompilerParams(dimension_semantics=("arbitrary",),
                                             vmem_limit_bytes=VMEM_LIMIT),
        name="combine",
    )(*[idx] * IDX_ARRAYS, w, _expansion_matrix(), table)


def _final_kernel(x1_ref, peer_ref, g_ref, out_ref):
    out_ref[...] = _rms(x1_ref[...] + peer_ref[...], g_ref[...])


def _final(x1, peer, g_final):
    n = x1.shape[0]
    spec = pl.BlockSpec((FINAL_TILE, D_MODEL), lambda i: (i, 0))
    return pl.pallas_call(
        _final_kernel,
        grid=(n // FINAL_TILE,),
        in_specs=[spec, spec, pl.BlockSpec((1, D_MODEL), lambda i: (0, 0))],
        out_specs=spec,
        out_shape=jax.ShapeDtypeStruct((n, D_MODEL), F32),
        compiler_params=pltpu.CompilerParams(dimension_semantics=("arbitrary",),
                                             vmem_limit_bytes=VMEM_LIMIT),
        name="final",
    )(x1, peer, g_final)


def kernel(x, g_mix, w_in, pool_group_w, pool_scale, conv_w, w_branch_pool, w_branch_conv, w_out, g_ffn,
           w_q, sub_keys, expert_u, expert_v, g_final):
    batch, seq, d = x.shape
    n = batch * seq
    assert g_mix.shape[0] == 1, "the final residual add is fused with the final norm: one layer only"
    l = 0
    x1 = _mixer(x.reshape(n, d), g_mix[l][None, :], w_in[l].astype(BF16), pool_group_w[l].astype(BF16),
                pool_scale[l][None, :], conv_w[l], w_branch_pool[l].astype(BF16),
                w_branch_conv[l].astype(BF16), w_out[l].astype(BF16), batch, seq)
    h2, idx, gates = _route(x1, g_ffn[l][None, :], w_q[l].astype(BF16), sub_keys[l].astype(BF16))
    h2r = h2.reshape(n, 2, ROW_SUBLANES, LANES).transpose(0, 2, 1, 3).reshape(n, SUBLANES, LANES)
    wts = _acts(idx, h2r, gates, _pack_table(expert_u[l]))
    peer = _combine(idx, wts, _pack_table(expert_v[l]))
    return _final(x1, peer.reshape(n, d), g_final[None, :]).reshape(batch, seq, d)
```

```python
import functools
import math

import jax
import jax.numpy as jnp
from jax import lax
from jax.experimental import pallas as pl
from jax.experimental.pallas import tpu as pltpu

D_MODEL = 1024
D_POOL = 512
POOL_WINDOWS = (2, 4, 8, 16)
POOL_GROUP_DIM = 128
D_CONV = 512
CONV_WIDTH = 3
N_KEYS = 128
N_EXPERTS = N_KEYS * N_KEYS
PEER_HEADS = 8
PEER_HALF = 128
PEER_TOPK = 16
SLOTS = PEER_HEADS * PEER_TOPK
RMS_EPS = 1e-6

LANES = 128
HALF_D = D_MODEL // 2
ROW_SUBLANES = HALF_D // LANES

MIX_TILE = 512
POOL_HIST = 16
CONV_HIST = 8
GATHER_TILE = 256
GROUP_TOKENS = GATHER_TILE // PEER_HEADS
FINAL_TILE = 1024
VMEM_LIMIT = 56 * 1024 * 1024

F32 = jnp.float32
BF16 = jnp.bfloat16


def _rms(x, g):
    return x * lax.rsqrt(jnp.mean(x * x, axis=-1, keepdims=True) + RMS_EPS) * g


def _dot(a, b):
    return jnp.dot(a, b, preferred_element_type=F32)


def _mixer_kernel(x_ref, g_ref, win_ref, pgw_ref, pscale_ref, convw_ref, wbp_ref, wbc_ref, wout_ref,
                  x1_ref, zp_buf, u_buf):
    s = pl.program_id(1)
    ts = MIX_TILE

    @pl.when(s == 0)
    def _():
        zp_buf[0:POOL_HIST, :] = jnp.zeros((POOL_HIST, D_POOL), F32)
        u_buf[0:CONV_HIST, :] = jnp.zeros((CONV_HIST, D_CONV), F32)

    x = x_ref[...]
    hb = _rms(x, g_ref[...]).astype(BF16)

    zp = _dot(hb, win_ref[:, 0:D_POOL])
    zp_buf[POOL_HIST:POOL_HIST + ts, :] = zp
    pos = s * ts + lax.broadcasted_iota(jnp.int32, (ts, 1), 0)
    outs = []
    for g, w in enumerate(POOL_WINDOWS):
        c0 = g * POOL_GROUP_DIM
        cur = zp[:, c0:c0 + POOL_GROUP_DIM]
        acc = cur
        for k in range(1, w):
            acc = acc + zp_buf[POOL_HIST - k:POOL_HIST - k + ts, c0:c0 + POOL_GROUP_DIM]
        cnt = jnp.minimum(pos + 1, w).astype(F32)
        pg = acc / cnt - cur
        og = _dot(pg.astype(BF16), pgw_ref[g]) * pscale_ref[:, c0:c0 + POOL_GROUP_DIM]
        outs.append(og)
    pm = jnp.concatenate(outs, axis=1)
    a = _dot(pm.astype(BF16), wbp_ref[...])
    zp_buf[0:POOL_HIST, :] = zp_buf[ts:ts + POOL_HIST, :]

    o = D_POOL
    zc = _dot(hb, win_ref[:, o:o + D_CONV])
    zb = _dot(hb, win_ref[:, o + D_CONV:o + 2 * D_CONV])
    zv = _dot(hb, win_ref[:, o + 2 * D_CONV:o + 3 * D_CONV])
    u = zc * zv
    u_buf[CONV_HIST:CONV_HIST + ts, :] = u
    y = (u_buf[CONV_HIST - 2:CONV_HIST - 2 + ts, :] * convw_ref[0:1, :]
         + u_buf[CONV_HIST - 1:CONV_HIST - 1 + ts, :] * convw_ref[1:2, :]
         + u * convw_ref[2:3, :])
    b = _dot((zb * y).astype(BF16), wbc_ref[...])
    u_buf[0:CONV_HIST, :] = u_buf[ts:ts + CONV_HIST, :]

    o = D_POOL + 3 * D_CONV
    ga = _dot(hb, win_ref[:, o:o + D_MODEL])
    gb = _dot(hb, win_ref[:, o + D_MODEL:o + 2 * D_MODEL])
    m = jax.nn.sigmoid(ga) * a + jax.nn.sigmoid(gb) * b
    x1_ref[...] = x + _dot(m.astype(BF16), wout_ref[...])


def _const_spec(shape):
    nd = len(shape)
    return pl.BlockSpec(shape, lambda *_: (0,) * nd, pipeline_mode=pl.Buffered(1))


def _mixer(x2d, g_mix, w_in, pgw, pscale, conv_w, wbp, wbc, w_out, batch, seq):
    n = batch * seq
    n_s = seq // MIX_TILE
    d_in = w_in.shape[1]
    return pl.pallas_call(
        _mixer_kernel,
        grid=(batch, n_s),
        in_specs=[
            pl.BlockSpec((MIX_TILE, D_MODEL), lambda b, s: (b * n_s + s, 0)),
            _const_spec((1, D_MODEL)),
            _const_spec((D_MODEL, d_in)),
            _const_spec((len(POOL_WINDOWS), POOL_GROUP_DIM, POOL_GROUP_DIM)),
            _const_spec((1, D_POOL)),
            _const_spec((CONV_WIDTH, D_CONV)),
            _const_spec((D_POOL, D_MODEL)),
            _const_spec((D_CONV, D_MODEL)),
            _const_spec((D_MODEL, D_MODEL)),
        ],
        out_specs=pl.BlockSpec((MIX_TILE, D_MODEL), lambda b, s: (b * n_s + s, 0)),
        out_shape=jax.ShapeDtypeStruct((n, D_MODEL), F32),
        scratch_shapes=[pltpu.VMEM((POOL_HIST + MIX_TILE, D_POOL), F32),
                        pltpu.VMEM((CONV_HIST + MIX_TILE, D_CONV), F32)],
        compiler_params=pltpu.CompilerParams(dimension_semantics=("arbitrary", "arbitrary"),
                                             vmem_limit_bytes=VMEM_LIMIT),
        name="mixer",
    )(x2d, g_mix, w_in, pgw, pscale, conv_w, wbp, wbc, w_out)


def _topk_axis0(s, k, pos, payload=None, pace=None):
    vals, sel = [], []
    for r in range(k):
        if pace is not None:
            s = jnp.concatenate([pace(r, s[0:8]), s[8:]], axis=0)
        m = jnp.max(s, axis=0, keepdims=True)
        p = jnp.min(jnp.where(s == m, pos, jnp.inf), axis=0, keepdims=True)
        hit = pos == p
        vals.append(m)
        if payload is None:
            sel.append(p)
        else:
            sel.append(jnp.max(jnp.where(hit, payload, -1.0), axis=0, keepdims=True))
        s = jnp.where(hit, -jnp.inf, s)
    return jnp.concatenate(vals, axis=0), jnp.concatenate(sel, axis=0)


_CAND_GROUPS = ((range(0, 1), range(0, 16)),) + tuple((range(i, i + 1), range(0, 8)) for i in range(1, 8)) \
    + ((range(8, 16), range(0, 1)),)


def _candidates(top_s, top_i):
    t = top_s[0].shape[1]
    cs, cp, ci = [], [], []
    for i_rng, j_rng in _CAND_GROUPS:
        i0, i1, j0, j1 = i_rng.start, i_rng.stop, j_rng.start, j_rng.stop
        rows = max(i1 - i0, j1 - j0)
        r = lax.broadcasted_iota(jnp.int32, (rows, t), 0).astype(F32)
        cs.append(top_s[0][i0:i1, :] + top_s[1][j0:j1, :])
        ci.append(top_i[0][i0:i1, :] * N_KEYS + top_i[1][j0:j1, :])
        cp.append(r * PEER_TOPK + i0 * PEER_TOPK + j0 if i1 - i0 > 1 else r + i0 * PEER_TOPK + j0)
    return jnp.concatenate(cs, axis=0), jnp.concatenate(cp, axis=0), jnp.concatenate(ci, axis=0)


def _route_scores(hb, wq, keys):
    q = _dot(hb, wq).astype(BF16)
    return [[lax.dot_general(keys[p], q[c * LANES:(c + 1) * LANES, p * PEER_HALF:(p + 1) * PEER_HALF],
                             (((1,), (1,)), ((), ())), preferred_element_type=F32) for p in range(2)]
            for c in range(hb.shape[0] // LANES)]


def _route_select(scores, store, pace=None):
    for c, halves in enumerate(scores):
        tok = slice(c * LANES, (c + 1) * LANES)
        key_pos = lax.broadcasted_iota(jnp.int32, (N_KEYS, LANES), 0).astype(F32)
        paces = [None, None] if pace is None else [functools.partial(pace, stage, c) for stage in range(2)]
        top_s, top_i = [], []
        for st in halves:
            v, i = _topk_axis0(st, PEER_TOPK, key_pos, pace=paces[0])
            top_s.append(v)
            top_i.append(i)
        cand_s, cand_pos, cand_i = _candidates(top_s, top_i)
        fin_s, fin_i = _topk_axis0(cand_s, PEER_TOPK, cand_pos, payload=cand_i, pace=paces[1])
        e = jnp.exp(fin_s - fin_s[0:1, :])
        gates = e / jnp.sum(e, axis=0, keepdims=True)
        store(tok, (fin_i * ROW_SUBLANES).astype(jnp.int32), gates)


def _pack_table(w):
    bits = lax.bitcast_convert_type(w.astype(BF16), jnp.uint16).astype(jnp.uint32)
    word = bits[:, :HALF_D] | (bits[:, HALF_D:] << 16)
    return word.reshape(w.shape[0] * ROW_SUBLANES, LANES)


def _gelu(x):
    return 0.5 * x * (1.0 + lax.erf(x * (1.0 / math.sqrt(2.0))))


SUBLANES = 2 * ROW_SUBLANES


IDX_GROUP = 8
IDX_ARRAYS = SLOTS // IDX_GROUP


def _pair_rows(tab_ref, idx_refs, t, ja, jb):
    def row(j):
        off = idx_refs[j // IDX_GROUP][j % IDX_GROUP, t]
        return tab_ref[pl.ds(pl.multiple_of(off, ROW_SUBLANES), ROW_SUBLANES), :]
    return jnp.concatenate([row(ja), row(jb)], axis=0)


PAIRS = SLOTS // 2
WORD_ROWS = PAIRS * SUBLANES
HALF_ROWS = 2 * WORD_ROWS


def _stacked_rows(tab_ref, idx_refs, t):
    z = jnp.concatenate([_pair_rows(tab_ref, idx_refs, t, 2 * k, 2 * k + 1) for k in range(PAIRS)], axis=0)
    return pltpu.bitcast(z, BF16)


def _slot_sum_matrix():
    c = jnp.arange(HALF_ROWS)
    return (c[:, None] // SUBLANES == jnp.arange(SLOTS)[None, :]).astype(BF16)


def _idx_copies(idx_buf, idx_smem, sem):
    return [pltpu.make_async_copy(idx_buf.at[pl.ds(a * IDX_GROUP, IDX_GROUP), :], idx_smem[a], sem.at[a])
            for a in range(IDX_ARRAYS)]


def _route_acts_kernel(x1_ref, g_ref, wq_ref, keys_ref, ssum_ref, tab_ref, idx_ref, w_ref, *scratch):
    idx_smem, (hb_buf, h_buf, gate_buf, idx_buf, dots_buf, sem) = scratch[:IDX_ARRAYS], scratch[IDX_ARRAYS:]
    step = pl.program_id(0)
    cur = step % 2
    prev = 1 - cur
    row = lax.broadcasted_iota(jnp.int32, (SUBLANES, HALF_ROWS), 0)
    col = lax.broadcasted_iota(jnp.int32, (SUBLANES, HALF_ROWS), 1)
    own = row == (col & (SUBLANES - 1))
    never = col[:, 0:LANES] < -1 - step

    @pl.when(step == 0)
    def _():
        idx_buf[...] = jnp.zeros(idx_buf.shape, jnp.int32)
        h_buf[1] = jnp.zeros(h_buf.shape[1:], F32)
        gate_buf[1] = jnp.zeros(gate_buf.shape[1:], F32)
        for cp in _idx_copies(idx_buf, idx_smem, sem):
            cp.start()
        for cp in _idx_copies(idx_buf, idx_smem, sem):
            cp.wait()

    h2 = _rms(x1_ref[...], g_ref[...])
    h_buf[cur] = h2
    hb_buf[...] = h2.astype(BF16)

    def chunk_rows(h8, i):
        return jnp.concatenate(
            [h8[i:i + 1, (ROW_SUBLANES * (n % 2) + n // 2) * LANES:(ROW_SUBLANES * (n % 2) + n // 2 + 1) * LANES]
             for n in range(SUBLANES)], axis=0).astype(BF16)

    def token_dots(t, lhs):
        d = lax.dot_general(lhs, _stacked_rows(tab_ref, idx_smem, t), (((1,), (1,)), ((), ())),
                            preferred_element_type=F32)
        return jnp.sum(jnp.where(own, d, 0.0), axis=0, keepdims=True)

    def store_head(g, tok, offs, gates):
        rows = pl.ds(pl.multiple_of(g * PEER_TOPK, PEER_TOPK), PEER_TOPK)
        idx_buf[rows, tok] = offs
        gate_buf[cur, rows, tok] = gates

    def body(g, carry):
        scores = _route_scores(hb_buf[...], wq_ref[g], keys_ref[g])
        base = pl.multiple_of(g * GROUP_TOKENS, GROUP_TOKENS)
        dots = []
        for j in range(GROUP_TOKENS // SUBLANES):
            h8 = h_buf[prev, pl.ds(base + j * SUBLANES, SUBLANES), :]
            dots += [token_dots(base + j * SUBLANES + i, chunk_rows(h8, i)) for i in range(SUBLANES)]
        dots_buf[pl.ds(base, GROUP_TOKENS), :] = jnp.concatenate(dots, axis=0)

        def pace(stage, chunk, r, tile):
            n_chunks = GATHER_TILE // LANES
            k = ((stage * n_chunks + chunk) * PEER_TOPK + r) * GROUP_TOKENS // (2 * n_chunks * PEER_TOPK)
            return jnp.where(never, jnp.broadcast_to(dots[k][:, 0:LANES], tile.shape), tile)

        _route_select(scores, functools.partial(store_head, g), pace)
        return carry

    lax.fori_loop(0, PEER_HEADS, body, 0)

    v = dots_buf[...]
    hi = v.astype(BF16)
    r1 = v - hi.astype(F32)
    mid = r1.astype(BF16)
    lo = (r1 - mid.astype(F32)).astype(BF16)
    act = (_dot(hi, ssum_ref[...]) + _dot(mid, ssum_ref[...])) + _dot(lo, ssum_ref[...])
    w_ref[0] = gate_buf[prev] * _gelu(act.T)

    idx_ref[0] = idx_buf[...]
    for cp in _idx_copies(idx_buf, idx_smem, sem):
        cp.start()
    for cp in _idx_copies(idx_buf, idx_smem, sem):
        cp.wait()


def _slot_major_spec(index_map):
    return pl.BlockSpec((1, SLOTS, GATHER_TILE), index_map)


def _route_acts(x1, g_ffn, w_q, keys, table):
    n = x1.shape[0]
    nb = n // GATHER_TILE
    assert GATHER_TILE // GROUP_TOKENS == PEER_HEADS and GROUP_TOKENS % SUBLANES == 0
    routed = lambda i: jnp.minimum(i, nb - 1)
    gathered = lambda i: jnp.maximum(i - 1, 0)
    return pl.pallas_call(
        _route_acts_kernel,
        grid=(nb + 1,),
        in_specs=[
            pl.BlockSpec((GATHER_TILE, D_MODEL), lambda i: (routed(i), 0)),
            _const_spec((1, D_MODEL)),
            _const_spec((PEER_HEADS, D_MODEL, 2 * PEER_HALF)),
            _const_spec((PEER_HEADS, 2, N_KEYS, PEER_HALF)),
            _const_spec((HALF_ROWS, SLOTS)),
            pl.BlockSpec(memory_space=pltpu.VMEM),
        ],
        out_specs=[_slot_major_spec(lambda i: (routed(i), 0, 0)), _slot_major_spec(lambda i: (gathered(i), 0, 0))],
        out_shape=[jax.ShapeDtypeStruct((nb, SLOTS, GATHER_TILE), jnp.int32),
                   jax.ShapeDtypeStruct((nb, SLOTS, GATHER_TILE), F32)],
        scratch_shapes=[pltpu.SMEM((IDX_GROUP, GATHER_TILE), jnp.int32)] * IDX_ARRAYS + [
            pltpu.VMEM((GATHER_TILE, D_MODEL), BF16),
            pltpu.VMEM((2, GATHER_TILE, D_MODEL), F32),
            pltpu.VMEM((2, SLOTS, GATHER_TILE), F32),
            pltpu.VMEM((SLOTS, GATHER_TILE), jnp.int32),
            pltpu.VMEM((GATHER_TILE, HALF_ROWS), F32),
            pltpu.SemaphoreType.DMA((IDX_ARRAYS,)),
        ],
        compiler_params=pltpu.CompilerParams(dimension_semantics=("arbitrary",),
                                             vmem_limit_bytes=VMEM_LIMIT),
        name="route_acts",
    )(x1, g_ffn, w_q, keys, _slot_sum_matrix(), table)


def _smem_idx_specs():
    return [pl.BlockSpec((None, IDX_GROUP, GATHER_TILE), functools.partial(lambda a, i: (i, a, 0), a),
                         memory_space=pltpu.SMEM, pipeline_mode=pl.Buffered(1))
            for a in range(IDX_ARRAYS)]


def _expansion_matrix():
    return _slot_sum_matrix().T


def _combine_kernel(*refs):
    idx_refs, (w_ref, expand_ref, tab_ref, out_ref, wexp_buf) = refs[:IDX_ARRAYS], refs[IDX_ARRAYS:]
    row = lax.broadcasted_iota(jnp.int32, (SUBLANES, LANES), 0)
    col = lax.broadcasted_iota(jnp.int32, (SUBLANES, LANES), 1)
    own = row == ROW_SUBLANES * (col & 1) + ((col >> 1) & (ROW_SUBLANES - 1))
    wexp_buf[...] = lax.dot_general(w_ref[0].astype(BF16), expand_ref[...], (((0,), (0,)), ((), ())),
                                    preferred_element_type=F32)

    def token(t, wrow):
        zb = _stacked_rows(tab_ref, idx_refs, t)
        lhs = jnp.concatenate(
            [jnp.where(own, jnp.broadcast_to(wrow[:, i * LANES:(i + 1) * LANES], (SUBLANES, LANES)), 0.0)
             for i in range(HALF_ROWS // LANES)], axis=1).astype(BF16)
        out_ref[t] = jnp.dot(lhs, zb, preferred_element_type=F32)

    def group(g, carry):
        base = pl.multiple_of(g * GROUP_TOKENS, GROUP_TOKENS)
        wg = wexp_buf[pl.ds(base, GROUP_TOKENS), :]
        for i in range(GROUP_TOKENS):
            token(base + i, wg[i:i + 1, :])
        return carry

    lax.fori_loop(0, GATHER_TILE // GROUP_TOKENS, group, 0)


def _combine(idx, w, table):
    n = w.shape[0] * w.shape[2]
    return pl.pallas_call(
        _combine_kernel,
        grid=(n // GATHER_TILE,),
        in_specs=_smem_idx_specs() + [_slot_major_spec(lambda i: (i, 0, 0)), _const_spec((SLOTS, HALF_ROWS)),
                                      pl.BlockSpec(memory_space=pltpu.VMEM)],
        out_specs=pl.BlockSpec((GATHER_TILE, 2 * ROW_SUBLANES, LANES), lambda i: (i, 0, 0)),
        out_shape=jax.ShapeDtypeStruct((n, 2 * ROW_SUBLANES, LANES), F32),
        scratch_shapes=[pltpu.VMEM((GATHER_TILE, HALF_ROWS), F32)],
        compiler_params=pltpu.CompilerParams(dimension_semantics=("arbitrary",),
                                             vmem_limit_bytes=VMEM_LIMIT),
        name="combine",
    )(*[idx] * IDX_ARRAYS, w, _expansion_matrix(), table)


def _final_kernel(x1_ref, peer_ref, g_ref, out_ref):
    out_ref[...] = _rms(x1_ref[...] + peer_ref[...], g_ref[...])


def _final(x1, peer, g_final):
    n = x1.shape[0]
    spec = pl.BlockSpec((FINAL_TILE, D_MODEL), lambda i: (i, 0))
    return pl.pallas_call(
        _final_kernel,
        grid=(n // FINAL_TILE,),
        in_specs=[spec, spec, pl.BlockSpec((1, D_MODEL), lambda i: (0, 0))],
        out_specs=spec,
        out_shape=jax.ShapeDtypeStruct((n, D_MODEL), F32),
        compiler_params=pltpu.CompilerParams(dimension_semantics=("arbitrary",),
                                             vmem_limit_bytes=VMEM_LIMIT),
        name="final",
    )(x1, peer, g_final)


def kernel(x, g_mix, w_in, pool_group_w, pool_scale, conv_w, w_branch_pool, w_branch_conv, w_out, g_ffn,
           w_q, sub_keys, expert_u, expert_v, g_final):
    batch, seq, d = x.shape
    n = batch * seq
    assert g_mix.shape[0] == 1, "the final residual add is fused with the final norm: one layer only"
    l = 0
    x1 = _mixer(x.reshape(n, d), g_mix[l][None, :], w_in[l].astype(BF16), pool_group_w[l].astype(BF16),
                pool_scale[l][None, :], conv_w[l], w_branch_pool[l].astype(BF16),
                w_branch_conv[l].astype(BF16), w_out[l].astype(BF16), batch, seq)
    w_q_heads = w_q[l].astype(BF16).reshape(d, PEER_HEADS, 2 * PEER_HALF).transpose(1, 0, 2)
    idx, wts = _route_acts(x1, g_ffn[l][None, :], w_q_heads, sub_keys[l].astype(BF16), _pack_table(expert_u[l]))
    peer = _combine(idx, wts, _pack_table(expert_v[l]))
    return _final(x1, peer.reshape(n, d), g_final[None, :]).reshape(batch, seq, d)
```

```python
import functools
import math

import jax
import jax.numpy as jnp
from jax import lax
from jax.experimental import pallas as pl
from jax.experimental.pallas import tpu as pltpu

D_MODEL = 1024
D_POOL = 512
POOL_WINDOWS = (2, 4, 8, 16)
POOL_GROUP_DIM = 128
D_CONV = 512
CONV_WIDTH = 3
N_KEYS = 128
N_EXPERTS = N_KEYS * N_KEYS
PEER_HEADS = 8
PEER_HALF = 128
PEER_TOPK = 16
SLOTS = PEER_HEADS * PEER_TOPK
RMS_EPS = 1e-6

LANES = 128
HALF_D = D_MODEL // 2
ROW_SUBLANES = HALF_D // LANES

MIX_TILE = 512
POOL_HIST = 16
CONV_HIST = 8
GATHER_TILE = 256
GROUP_TOKENS = GATHER_TILE // PEER_HEADS
FINAL_TILE = 1024
VMEM_LIMIT = 56 * 1024 * 1024

F32 = jnp.float32
BF16 = jnp.bfloat16


def _rms(x, g):
    return x * lax.rsqrt(jnp.mean(x * x, axis=-1, keepdims=True) + RMS_EPS) * g


def _dot(a, b):
    return jnp.dot(a, b, preferred_element_type=F32)


def _mixer_kernel(x_ref, g_ref, win_ref, pgw_ref, pscale_ref, convw_ref, wbp_ref, wbc_ref, wout_ref,
                  x1_ref, zp_buf, u_buf):
    s = pl.program_id(1)
    ts = MIX_TILE

    @pl.when(s == 0)
    def _():
        zp_buf[0:POOL_HIST, :] = jnp.zeros((POOL_HIST, D_POOL), F32)
        u_buf[0:CONV_HIST, :] = jnp.zeros((CONV_HIST, D_CONV), F32)

    x = x_ref[...]
    hb = _rms(x, g_ref[...]).astype(BF16)

    zp = _dot(hb, win_ref[:, 0:D_POOL])
    zp_buf[POOL_HIST:POOL_HIST + ts, :] = zp
    pos = s * ts + lax.broadcasted_iota(jnp.int32, (ts, 1), 0)
    outs = []
    for g, w in enumerate(POOL_WINDOWS):
        c0 = g * POOL_GROUP_DIM
        cur = zp[:, c0:c0 + POOL_GROUP_DIM]
        acc = cur
        for k in range(1, w):
            acc = acc + zp_buf[POOL_HIST - k:POOL_HIST - k + ts, c0:c0 + POOL_GROUP_DIM]
        cnt = jnp.minimum(pos + 1, w).astype(F32)
        pg = acc / cnt - cur
        og = _dot(pg.astype(BF16), pgw_ref[g]) * pscale_ref[:, c0:c0 + POOL_GROUP_DIM]
        outs.append(og)
    pm = jnp.concatenate(outs, axis=1)
    a = _dot(pm.astype(BF16), wbp_ref[...])
    zp_buf[0:POOL_HIST, :] = zp_buf[ts:ts + POOL_HIST, :]

    o = D_POOL
    zc = _dot(hb, win_ref[:, o:o + D_CONV])
    zb = _dot(hb, win_ref[:, o + D_CONV:o + 2 * D_CONV])
    zv = _dot(hb, win_ref[:, o + 2 * D_CONV:o + 3 * D_CONV])
    u = zc * zv
    u_buf[CONV_HIST:CONV_HIST + ts, :] = u
    y = (u_buf[CONV_HIST - 2:CONV_HIST - 2 + ts, :] * convw_ref[0:1, :]
         + u_buf[CONV_HIST - 1:CONV_HIST - 1 + ts, :] * convw_ref[1:2, :]
         + u * convw_ref[2:3, :])
    b = _dot((zb * y).astype(BF16), wbc_ref[...])
    u_buf[0:CONV_HIST, :] = u_buf[ts:ts + CONV_HIST, :]

    o = D_POOL + 3 * D_CONV
    ga = _dot(hb, win_ref[:, o:o + D_MODEL])
    gb = _dot(hb, win_ref[:, o + D_MODEL:o + 2 * D_MODEL])
    m = jax.nn.sigmoid(ga) * a + jax.nn.sigmoid(gb) * b
    x1_ref[...] = x + _dot(m.astype(BF16), wout_ref[...])


def _const_spec(shape):
    nd = len(shape)
    return pl.BlockSpec(shape, lambda *_: (0,) * nd, pipeline_mode=pl.Buffered(1))


def _mixer(x2d, g_mix, w_in, pgw, pscale, conv_w, wbp, wbc, w_out, batch, seq):
    n = batch * seq
    n_s = seq // MIX_TILE
    d_in = w_in.shape[1]
    return pl.pallas_call(
        _mixer_kernel,
        grid=(batch, n_s),
        in_specs=[
            pl.BlockSpec((MIX_TILE, D_MODEL), lambda b, s: (b * n_s + s, 0)),
            _const_spec((1, D_MODEL)),
            _const_spec((D_MODEL, d_in)),
            _const_spec((len(POOL_WINDOWS), POOL_GROUP_DIM, POOL_GROUP_DIM)),
            _const_spec((1, D_POOL)),
            _const_spec((CONV_WIDTH, D_CONV)),
            _const_spec((D_POOL, D_MODEL)),
            _const_spec((D_CONV, D_MODEL)),
            _const_spec((D_MODEL, D_MODEL)),
        ],
        out_specs=pl.BlockSpec((MIX_TILE, D_MODEL), lambda b, s: (b * n_s + s, 0)),
        out_shape=jax.ShapeDtypeStruct((n, D_MODEL), F32),
        scratch_shapes=[pltpu.VMEM((POOL_HIST + MIX_TILE, D_POOL), F32),
                        pltpu.VMEM((CONV_HIST + MIX_TILE, D_CONV), F32)],
        compiler_params=pltpu.CompilerParams(dimension_semantics=("arbitrary", "arbitrary"),
                                             vmem_limit_bytes=VMEM_LIMIT),
        name="mixer",
    )(x2d, g_mix, w_in, pgw, pscale, conv_w, wbp, wbc, w_out)


def _topk_axis0(s, k, pos, payload=None, pace=None):
    vals, sel = [], []
    for r in range(k):
        if pace is not None:
            s = jnp.concatenate([pace(r, s[0:8]), s[8:]], axis=0)
        m = jnp.max(s, axis=0, keepdims=True)
        p = jnp.min(jnp.where(s == m, pos, jnp.inf), axis=0, keepdims=True)
        hit = pos == p
        vals.append(m)
        if payload is None:
            sel.append(p)
        else:
            sel.append(jnp.max(jnp.where(hit, payload, -1.0), axis=0, keepdims=True))
        s = jnp.where(hit, -jnp.inf, s)
    return jnp.concatenate(vals, axis=0), jnp.concatenate(sel, axis=0)


_CAND_GROUPS = ((range(0, 1), range(0, 16)),) + tuple((range(i, i + 1), range(0, 8)) for i in range(1, 8)) \
    + ((range(8, 16), range(0, 1)),)


def _candidates(top_s, top_i):
    t = top_s[0].shape[1]
    cs, cp, ci = [], [], []
    for i_rng, j_rng in _CAND_GROUPS:
        i0, i1, j0, j1 = i_rng.start, i_rng.stop, j_rng.start, j_rng.stop
        rows = max(i1 - i0, j1 - j0)
        r = lax.broadcasted_iota(jnp.int32, (rows, t), 0).astype(F32)
        cs.append(top_s[0][i0:i1, :] + top_s[1][j0:j1, :])
        ci.append(top_i[0][i0:i1, :] * N_KEYS + top_i[1][j0:j1, :])
        cp.append(r * PEER_TOPK + i0 * PEER_TOPK + j0 if i1 - i0 > 1 else r + i0 * PEER_TOPK + j0)
    return jnp.concatenate(cs, axis=0), jnp.concatenate(cp, axis=0), jnp.concatenate(ci, axis=0)


def _route_scores(hb, wq, keys):
    q = _dot(hb, wq).astype(BF16)
    return [[lax.dot_general(keys[p], q[c * LANES:(c + 1) * LANES, p * PEER_HALF:(p + 1) * PEER_HALF],
                             (((1,), (1,)), ((), ())), preferred_element_type=F32) for p in range(2)]
            for c in range(hb.shape[0] // LANES)]


def _route_select(scores, store, pace=None):
    for c, halves in enumerate(scores):
        tok = slice(c * LANES, (c + 1) * LANES)
        key_pos = lax.broadcasted_iota(jnp.int32, (N_KEYS, LANES), 0).astype(F32)
        paces = [None] * 3 if pace is None else [functools.partial(pace, 3 * c + u) for u in range(3)]
        top_s, top_i = [], []
        for st, half_pace in zip(halves, paces):
            v, i = _topk_axis0(st, PEER_TOPK, key_pos, pace=half_pace)
            top_s.append(v)
            top_i.append(i)
        cand_s, cand_pos, cand_i = _candidates(top_s, top_i)
        fin_s, fin_i = _topk_axis0(cand_s, PEER_TOPK, cand_pos, payload=cand_i, pace=paces[2])
        e = jnp.exp(fin_s - fin_s[0:1, :])
        gates = e / jnp.sum(e, axis=0, keepdims=True)
        store(tok, (fin_i * ROW_SUBLANES).astype(jnp.int32), gates)


def _pack_table(w):
    bits = lax.bitcast_convert_type(w.astype(BF16), jnp.uint16).astype(jnp.uint32)
    word = bits[:, :HALF_D] | (bits[:, HALF_D:] << 16)
    return word.reshape(w.shape[0] * ROW_SUBLANES, LANES)


def _gelu(x):
    return 0.5 * x * (1.0 + lax.erf(x * (1.0 / math.sqrt(2.0))))


SUBLANES = 2 * ROW_SUBLANES


IDX_GROUP = 8
IDX_ARRAYS = SLOTS // IDX_GROUP


def _pair_rows(tab_ref, idx_refs, t, ja, jb):
    def row(j):
        off = idx_refs[j // IDX_GROUP][j % IDX_GROUP, t]
        return tab_ref[pl.ds(pl.multiple_of(off, ROW_SUBLANES), ROW_SUBLANES), :]
    return jnp.concatenate([row(ja), row(jb)], axis=0)


PAIRS = SLOTS // 2
WORD_ROWS = PAIRS * SUBLANES
HALF_ROWS = 2 * WORD_ROWS


def _stacked_rows(tab_ref, idx_refs, t):
    z = jnp.concatenate([_pair_rows(tab_ref, idx_refs, t, 2 * k, 2 * k + 1) for k in range(PAIRS)], axis=0)
    return pltpu.bitcast(z, BF16)


def _slot_sum_matrix():
    c = jnp.arange(HALF_ROWS)
    return (c[:, None] // SUBLANES == jnp.arange(SLOTS)[None, :]).astype(BF16)


def _idx_copies(idx_buf, idx_smem, sem):
    return [pltpu.make_async_copy(idx_buf.at[pl.ds(a * IDX_GROUP, IDX_GROUP), :], idx_smem[a], sem.at[a])
            for a in range(IDX_ARRAYS)]


def _route_acts_kernel(x1_ref, g_ref, wq_ref, keys_ref, ssum_ref, tab_ref, idx_ref, w_ref, *scratch):
    idx_smem, (hb_buf, h_buf, gate_buf, idx_buf, dots_buf, sem) = scratch[:IDX_ARRAYS], scratch[IDX_ARRAYS:]
    step = pl.program_id(0)
    cur = step % 2
    prev = 1 - cur
    row = lax.broadcasted_iota(jnp.int32, (SUBLANES, HALF_ROWS), 0)
    col = lax.broadcasted_iota(jnp.int32, (SUBLANES, HALF_ROWS), 1)
    own = row == (col & (SUBLANES - 1))
    never = col[:, 0:LANES] < -1 - step

    @pl.when(step == 0)
    def _():
        idx_buf[...] = jnp.zeros(idx_buf.shape, jnp.int32)
        h_buf[1] = jnp.zeros(h_buf.shape[1:], F32)
        gate_buf[1] = jnp.zeros(gate_buf.shape[1:], F32)
        for cp in _idx_copies(idx_buf, idx_smem, sem):
            cp.start()
        for cp in _idx_copies(idx_buf, idx_smem, sem):
            cp.wait()

    h2 = _rms(x1_ref[...], g_ref[...])
    h_buf[cur] = h2
    hb_buf[...] = h2.astype(BF16)

    def chunk_rows(h8, i):
        return jnp.concatenate(
            [h8[i:i + 1, (ROW_SUBLANES * (n % 2) + n // 2) * LANES:(ROW_SUBLANES * (n % 2) + n // 2 + 1) * LANES]
             for n in range(SUBLANES)], axis=0).astype(BF16)

    def token_dots(t, lhs):
        d = lax.dot_general(lhs, _stacked_rows(tab_ref, idx_smem, t), (((1,), (1,)), ((), ())),
                            preferred_element_type=F32)
        return jnp.sum(jnp.where(own, d, 0.0), axis=0, keepdims=True)

    def store_head(g, tok, offs, gates):
        rows = pl.ds(pl.multiple_of(g * PEER_TOPK, PEER_TOPK), PEER_TOPK)
        idx_buf[rows, tok] = offs
        gate_buf[cur, rows, tok] = gates

    def body(g, carry):
        scores = _route_scores(hb_buf[...], wq_ref[g], keys_ref[g])
        base = pl.multiple_of(g * GROUP_TOKENS, GROUP_TOKENS)
        dots = []
        for j in range(GROUP_TOKENS // SUBLANES):
            h8 = h_buf[prev, pl.ds(base + j * SUBLANES, SUBLANES), :]
            dots += [token_dots(base + j * SUBLANES + i, chunk_rows(h8, i)) for i in range(SUBLANES)]
        dots_buf[pl.ds(base, GROUP_TOKENS), :] = jnp.concatenate(dots, axis=0)

        def pace(stream, r, tile):
            k = (stream * PEER_TOPK + r) * GROUP_TOKENS // (3 * (GATHER_TILE // LANES) * PEER_TOPK)
            return jnp.where(never, jnp.broadcast_to(dots[k][:, 0:LANES], tile.shape), tile)

        _route_select(scores, functools.partial(store_head, g), pace)
        return carry

    lax.fori_loop(0, PEER_HEADS, body, 0)

    v = dots_buf[...]
    hi = v.astype(BF16)
    r1 = v - hi.astype(F32)
    mid = r1.astype(BF16)
    lo = (r1 - mid.astype(F32)).astype(BF16)
    act = (_dot(hi, ssum_ref[...]) + _dot(mid, ssum_ref[...])) + _dot(lo, ssum_ref[...])
    w_ref[0] = gate_buf[prev] * _gelu(act.T)

    idx_ref[0] = idx_buf[...]
    for cp in _idx_copies(idx_buf, idx_smem, sem):
        cp.start()
    for cp in _idx_copies(idx_buf, idx_smem, sem):
        cp.wait()


def _slot_major_spec(index_map):
    return pl.BlockSpec((1, SLOTS, GATHER_TILE), index_map)


def _route_acts(x1, g_ffn, w_q, keys, table):
    n = x1.shape[0]
    nb = n // GATHER_TILE
    assert GATHER_TILE // GROUP_TOKENS == PEER_HEADS and GROUP_TOKENS % SUBLANES == 0
    routed = lambda i: jnp.minimum(i, nb - 1)
    gathered = lambda i: jnp.maximum(i - 1, 0)
    return pl.pallas_call(
        _route_acts_kernel,
        grid=(nb + 1,),
        in_specs=[
            pl.BlockSpec((GATHER_TILE, D_MODEL), lambda i: (routed(i), 0)),
            _const_spec((1, D_MODEL)),
            _const_spec((PEER_HEADS, D_MODEL, 2 * PEER_HALF)),
            _const_spec((PEER_HEADS, 2, N_KEYS, PEER_HALF)),
            _const_spec((HALF_ROWS, SLOTS)),
            pl.BlockSpec(memory_space=pltpu.VMEM),
        ],
        out_specs=[_slot_major_spec(lambda i: (routed(i), 0, 0)), _slot_major_spec(lambda i: (gathered(i), 0, 0))],
        out_shape=[jax.ShapeDtypeStruct((nb, SLOTS, GATHER_TILE), jnp.int32),
                   jax.ShapeDtypeStruct((nb, SLOTS, GATHER_TILE), F32)],
        scratch_shapes=[pltpu.SMEM((IDX_GROUP, GATHER_TILE), jnp.int32)] * IDX_ARRAYS + [
            pltpu.VMEM((GATHER_TILE, D_MODEL), BF16),
            pltpu.VMEM((2, GATHER_TILE, D_MODEL), F32),
            pltpu.VMEM((2, SLOTS, GATHER_TILE), F32),
            pltpu.VMEM((SLOTS, GATHER_TILE), jnp.int32),
            pltpu.VMEM((GATHER_TILE, HALF_ROWS), F32),
            pltpu.SemaphoreType.DMA((IDX_ARRAYS,)),
        ],
        compiler_params=pltpu.CompilerParams(dimension_semantics=("arbitrary",),
                                             vmem_limit_bytes=VMEM_LIMIT),
        name="route_acts",
    )(x1, g_ffn, w_q, keys, _slot_sum_matrix(), table)


def _smem_idx_specs():
    return [pl.BlockSpec((None, IDX_GROUP, GATHER_TILE), functools.partial(lambda a, i: (i, a, 0), a),
                         memory_space=pltpu.SMEM, pipeline_mode=pl.Buffered(1))
            for a in range(IDX_ARRAYS)]


def _expansion_matrix():
    return _slot_sum_matrix().T


def _combine_kernel(*refs):
    idx_refs, (w_ref, expand_ref, tab_ref, out_ref, wexp_buf) = refs[:IDX_ARRAYS], refs[IDX_ARRAYS:]
    row = lax.broadcasted_iota(jnp.int32, (SUBLANES, LANES), 0)
    col = lax.broadcasted_iota(jnp.int32, (SUBLANES, LANES), 1)
    own = row == ROW_SUBLANES * (col & 1) + ((col >> 1) & (ROW_SUBLANES - 1))
    wexp_buf[...] = lax.dot_general(w_ref[0].astype(BF16), expand_ref[...], (((0,), (0,)), ((), ())),
                                    preferred_element_type=F32)

    def token(t, wrow):
        zb = _stacked_rows(tab_ref, idx_refs, t)
        lhs = jnp.concatenate(
            [jnp.where(own, jnp.broadcast_to(wrow[:, i * LANES:(i + 1) * LANES], (SUBLANES, LANES)), 0.0)
             for i in range(HALF_ROWS // LANES)], axis=1).astype(BF16)
        out_ref[t] = jnp.dot(lhs, zb, preferred_element_type=F32)

    def group(g, carry):
        base = pl.multiple_of(g * GROUP_TOKENS, GROUP_TOKENS)
        wg = wexp_buf[pl.ds(base, GROUP_TOKENS), :]
        for i in range(GROUP_TOKENS):
            token(base + i, wg[i:i + 1, :])
        return carry

    lax.fori_loop(0, GATHER_TILE // GROUP_TOKENS, group, 0)


def _combine(idx, w, table):
    n = w.shape[0] * w.shape[2]
    return pl.pallas_call(
        _combine_kernel,
        grid=(n // GATHER_TILE,),
        in_specs=_smem_idx_specs() + [_slot_major_spec(lambda i: (i, 0, 0)), _const_spec((SLOTS, HALF_ROWS)),
                                      pl.BlockSpec(memory_space=pltpu.VMEM)],
        out_specs=pl.BlockSpec((GATHER_TILE, 2 * ROW_SUBLANES, LANES), lambda i: (i, 0, 0)),
        out_shape=jax.ShapeDtypeStruct((n, 2 * ROW_SUBLANES, LANES), F32),
        scratch_shapes=[pltpu.VMEM((GATHER_TILE, HALF_ROWS), F32)],
        compiler_params=pltpu.CompilerParams(dimension_semantics=("arbitrary",),
                                             vmem_limit_bytes=VMEM_LIMIT),
        name="combine",
    )(*[idx] * IDX_ARRAYS, w, _expansion_matrix(), table)


def _final_kernel(x1_ref, peer_ref, g_ref, out_ref):
    out_ref[...] = _rms(x1_ref[...] + peer_ref[...], g_ref[...])


def _final(x1, peer, g_final):
    n = x1.shape[0]
    spec = pl.BlockSpec((FINAL_TILE, D_MODEL), lambda i: (i, 0))
    return pl.pallas_call(
        _final_kernel,
        grid=(n // FINAL_TILE,),
        in_specs=[spec, spec, pl.BlockSpec((1, D_MODEL), lambda i: (0, 0))],
        out_specs=spec,
        out_shape=jax.ShapeDtypeStruct((n, D_MODEL), F32),
        compiler_params=pltpu.CompilerParams(dimension_semantics=("arbitrary",),
                                             vmem_limit_bytes=VMEM_LIMIT),
        name="final",
    )(x1, peer, g_final)


def kernel(x, g_mix, w_in, pool_group_w, pool_scale, conv_w, w_branch_pool, w_branch_conv, w_out, g_ffn,
           w_q, sub_keys, expert_u, expert_v, g_final):
    batch, seq, d = x.shape
    n = batch * seq
    assert g_mix.shape[0] == 1, "the final residual add is fused with the final norm: one layer only"
    l = 0
    x1 = _mixer(x.reshape(n, d), g_mix[l][None, :], w_in[l].astype(BF16), pool_group_w[l].astype(BF16),
                pool_scale[l][None, :], conv_w[l], w_branch_pool[l].astype(BF16),
                w_branch_conv[l].astype(BF16), w_out[l].astype(BF16), batch, seq)
    w_q_heads = w_q[l].astype(BF16).reshape(d, PEER_HEADS, 2 * PEER_HALF).transpose(1, 0, 2)
    idx, wts = _route_acts(x1, g_ffn[l][None, :], w_q_heads, sub_keys[l].astype(BF16), _pack_table(expert_u[l]))
    peer = _combine(idx, wts, _pack_table(expert_v[l]))
    return _final(x1, peer.reshape(n, d), g_final[None, :]).reshape(batch, seq, d)
```
